```python
import math
import jax, jax.numpy as jnp
from jax import lax
import numpy as np

D_MODEL = 1024
BATCH = 1
SEQ = 16384
DEPTH = 4

N_MIXERS = 4
N_MEM = 256
HEAD_DIM = 64
MEM_HEADS = 4
MEM_WIDTH = MEM_HEADS * HEAD_DIM
MLA_HEADS = 12
MLA_Q_LORA = 384
MLA_KV_LORA = 256
MLA_NOPE = 64
MLA_ROPE = 32
MLA_V = 64
ROPE_THETA = 10000.0
Q_BLOCK = 128
A_IN = MLA_Q_LORA + MLA_KV_LORA + MLA_ROPE + MEM_WIDTH
A_OUT = MLA_HEADS * MLA_V + MEM_WIDTH
DIL_GROUPS = ((128, 1), (512, 4), (2048, 16))
DIL_HEADS = 8
DIL_BLOCK = 64
DIL_QKV = len(DIL_GROUPS) * 3 * DIL_HEADS * HEAD_DIM
B_IN = DIL_QKV + MEM_WIDTH
B_OUT = DIL_HEADS * HEAD_DIM + MEM_WIDTH
ALIBI_MAX = 8.0
CONV_CH = 768
CONV_WIDTH = 31
C_IN = 2 * CONV_CH + MEM_WIDTH
C_OUT = CONV_CH + MEM_WIDTH
SC_CH = 768
SC_WIDTH = 3
D_IN = 3 * SC_CH + MEM_WIDTH
D_OUT = SC_CH + MEM_WIDTH
D_FF = 2816
EPS = 1e-6
NEG = -1e30

kernel_name = "hybrid_interleaved_mla_dilated_conformer_shortconv_encoder"


def _n_uses(m):
    return len(range(m, DEPTH, N_MIXERS))


def rms_norm(x, g):
    xf = x.astype(jnp.float32)
    y = xf * lax.rsqrt(jnp.mean(xf * xf, axis=-1, keepdims=True) + EPS)
    return (y * g.astype(jnp.float32)).astype(x.dtype)


def layer_norm(x, g, b):
    xf = x.astype(jnp.float32)
    mu = jnp.mean(xf, axis=-1, keepdims=True)
    var = jnp.mean(jnp.square(xf - mu), axis=-1, keepdims=True)
    y = (xf - mu) * lax.rsqrt(var + EPS)
    return (y * g.astype(jnp.float32) + b.astype(jnp.float32)).astype(x.dtype)


def swiglu(h, wg, wu, wd):
    return jnp.einsum('bsf,fd->bsd', jax.nn.silu(h @ wg) * (h @ wu), wd)


def depthwise_conv(u, w):
    width, ch = w.shape
    return lax.conv_general_dilated(
        u, w[:, None, :].astype(u.dtype), window_strides=(1,),
        padding=[(width // 2, width // 2)],
        dimension_numbers=('NWC', 'WIO', 'NWC'), feature_group_count=ch)


def alibi_slopes(n):
    return 2.0 ** (-ALIBI_MAX * (jnp.arange(n, dtype=jnp.float32) + 1.0) / n)


def apply_rope(t, cos, sin):
    if t.ndim == 4:
        cos, sin = cos[:, :, None, :], sin[:, :, None, :]
    half = t.shape[-1] // 2
    tf = t.astype(jnp.float32)
    t1, t2 = tf[..., :half], tf[..., half:]
    return jnp.concatenate([t1 * cos - t2 * sin, t1 * sin + t2 * cos], axis=-1).astype(t.dtype)


def memory_attention(q_mem, mem_n, w_kv):
    bsz, seq, _ = q_mem.shape
    kv = (mem_n @ w_kv).reshape(bsz, mem_n.shape[1], 2, MEM_HEADS, HEAD_DIM)
    k, v = kv[:, :, 0], kv[:, :, 1]
    q = q_mem.reshape(bsz, seq, MEM_HEADS, HEAD_DIM)
    s = jnp.einsum('bshd,bnhd->bhsn', q, k).astype(jnp.float32) * HEAD_DIM ** -0.5
    p = jax.nn.softmax(s, axis=-1).astype(v.dtype)
    return jnp.einsum('bhsn,bnhd->bshd', p, v).reshape(bsz, seq, MEM_WIDTH)


def mla_mixer(h, positions, w_in, q_g, kv_g, w_uq, w_ukv):
    bsz, seq, _ = h.shape
    z = h @ w_in
    o1 = MLA_Q_LORA
    o2 = o1 + MLA_KV_LORA
    o3 = o2 + MLA_ROPE
    c_q, c_kv, k_r, q_mem = z[..., :o1], z[..., o1:o2], z[..., o2:o3], z[..., o3:]
    half = MLA_ROPE // 2
    inv = ROPE_THETA ** (-jnp.arange(half, dtype=jnp.float32) / half)
    ang = positions.astype(jnp.float32)[..., None] * inv
    cos, sin = jnp.cos(ang), jnp.sin(ang)
    q = (rms_norm(c_q, q_g) @ w_uq).reshape(bsz, seq, MLA_HEADS, MLA_NOPE + MLA_ROPE)
    q_n, q_r = q[..., :MLA_NOPE], apply_rope(q[..., MLA_NOPE:], cos, sin)
    kv = (rms_norm(c_kv, kv_g) @ w_ukv).reshape(bsz, seq, MLA_HEADS, MLA_NOPE + MLA_V)
    k_n, v = kv[..., :MLA_NOPE], kv[..., MLA_NOPE:]
    k_r = apply_rope(k_r, cos, sin)
    scale = (MLA_NOPE + MLA_ROPE) ** -0.5
    nb = seq // Q_BLOCK
    qn_b = q_n.reshape(bsz, nb, Q_BLOCK, MLA_HEADS, MLA_NOPE).transpose(1, 0, 2, 3, 4)
    qr_b = q_r.reshape(bsz, nb, Q_BLOCK, MLA_HEADS, MLA_ROPE).transpose(1, 0, 2, 3, 4)

    def block(args):
        qn, qr = args
        s = (jnp.einsum('bqhd,bkhd->bhqk', qn, k_n)
             + jnp.einsum('bqhr,bkr->bhqk', qr, k_r)).astype(jnp.float32) * scale
        p = jax.nn.softmax(s, axis=-1).astype(v.dtype)
        return jnp.einsum('bhqk,bkhd->bqhd', p, v)

    o = lax.map(block, (qn_b, qr_b))
    o = o.transpose(1, 0, 2, 3, 4).reshape(bsz, seq, MLA_HEADS * MLA_V)
    return o, q_mem


def dilated_group(q, k, v, dilation, half, slopes):
    bsz, seq, nh, dh = q.shape
    sub_len = seq // dilation
    nb = -(-sub_len // DIL_BLOCK)
    lp = nb * DIL_BLOCK

    def sub(t):
        return t.reshape(bsz, sub_len, dilation, nh, dh).transpose(0, 2, 1, 3, 4)

    qs = jnp.pad(sub(q), ((0, 0), (0, 0), (0, lp - sub_len), (0, 0), (0, 0)))
    qs = qs.reshape(bsz, dilation, nb, DIL_BLOCK, nh, dh)

    def kv_blocks(t):
        t = jnp.pad(sub(t), ((0, 0), (0, 0), (DIL_BLOCK, lp - sub_len + DIL_BLOCK), (0, 0), (0, 0)))
        t = t.reshape(bsz, dilation, nb + 2, DIL_BLOCK, nh, dh)
        return jnp.concatenate([t[:, :, :-2], t[:, :, 1:-1], t[:, :, 2:]], axis=3)

    kb, vb = kv_blocks(k), kv_blocks(v)
    qi = jnp.arange(DIL_BLOCK)
    kj = jnp.arange(3 * DIL_BLOCK)
    rel = kj[None, :] - DIL_BLOCK - qi[:, None]
    key_idx = jnp.arange(nb)[:, None] * DIL_BLOCK - DIL_BLOCK + kj[None, :]
    mask = (jnp.abs(rel) <= half)[None] & ((key_idx >= 0) & (key_idx < sub_len))[:, None, :]
    dist = (jnp.abs(rel) * dilation).astype(jnp.float32)
    s = jnp.einsum('brnqhd,brnkhd->brnhqk', qs, kb).astype(jnp.float32) * dh ** -0.5
    s = s - slopes[:, None, None] * dist
    s = jnp.where(mask[:, None], s, NEG)
    lse = jax.nn.logsumexp(s, axis=-1)
    p = jnp.exp(s - lse[..., None]).astype(v.dtype)
    o = jnp.einsum('brnhqk,brnkhd->brnqhd', p, vb)
    o = o.reshape(bsz, dilation, lp, nh, dh)[:, :, :sub_len]
    o = o.transpose(0, 2, 1, 3, 4).reshape(bsz, seq, nh, dh)
    lse = lse.transpose(0, 1, 2, 4, 3).reshape(bsz, dilation, lp, nh)[:, :, :sub_len]
    lse = lse.transpose(0, 2, 1, 3).reshape(bsz, seq, nh)
    return o, lse


def dilated_mixer(h, w_in):
    bsz, seq, _ = h.shape
    z = h @ w_in
    n_g = len(DIL_GROUPS)
    zd = z[..., :DIL_QKV].reshape(bsz, seq, n_g, 3, DIL_HEADS, HEAD_DIM)
    q_mem = z[..., DIL_QKV:]
    slopes = alibi_slopes(n_g * DIL_HEADS).reshape(n_g, DIL_HEADS)
    outs, lses = [], []
    for g, (window, dilation) in enumerate(DIL_GROUPS):
        o, l = dilated_group(zd[:, :, g, 0], zd[:, :, g, 1], zd[:, :, g, 2],
                             dilation, window // (2 * dilation), slopes[g])
        outs.append(o)
        lses.append(l)
    wts = jax.nn.softmax(jnp.stack(lses, axis=0), axis=0)
    o = jnp.einsum('gbsh,gbshd->bshd', wts.astype(outs[0].dtype), jnp.stack(outs, axis=0))
    return o.reshape(bsz, seq, DIL_HEADS * HEAD_DIM), q_mem


def conformer_conv_mixer(h, w_in, conv_w, conv_b, ln_g, ln_b):
    z = h @ w_in
    a, gate, q_mem = z[..., :CONV_CH], z[..., CONV_CH:2 * CONV_CH], z[..., 2 * CONV_CH:]
    u = a * jax.nn.sigmoid(gate)
    u = depthwise_conv(u, conv_w) + conv_b
    u = jax.nn.silu(layer_norm(u, ln_g, ln_b))
    return u, q_mem


def short_conv_mixer(h, w_in, conv_w):
    z = h @ w_in
    bg, cg = z[..., :SC_CH], z[..., SC_CH:2 * SC_CH]
    hx, q_mem = z[..., 2 * SC_CH:3 * SC_CH], z[..., 3 * SC_CH:]
    return bg * depthwise_conv(cg * hx, conv_w), q_mem


def setup_inputs(seed: int = 0) -> dict:
    key = jax.random.key(seed)
    ks = jax.random.split(key, 32)
    f32 = jnp.float32

    def w(k, shape, fan_in):
        return jax.random.normal(k, shape, f32) * fan_in ** -0.5

    def gain(k, shape):
        return 1.0 + 0.05 * jax.random.normal(k, shape, f32)

    na, nb_, nc, nd = _n_uses(0), _n_uses(1), _n_uses(2), _n_uses(3)
    offset = jax.random.randint(ks[2], (BATCH, 1), 0, 4096, dtype=jnp.int32)
    positions = offset + jnp.arange(SEQ, dtype=jnp.int32)[None, :]
    return {
        "x": jax.random.normal(ks[0], (BATCH, SEQ, D_MODEL), f32),
        "mem": jax.random.normal(ks[1], (BATCH, N_MEM, D_MODEL), f32),
        "positions": positions,
        "norm_g": gain(ks[3], (DEPTH, 7, D_MODEL)),
        "ffn_w_gate": w(ks[4], (DEPTH, 2, D_MODEL, D_FF), D_MODEL),
        "ffn_w_up": w(ks[5], (DEPTH, 2, D_MODEL, D_FF), D_MODEL),
        "ffn_w_down": w(ks[6], (DEPTH, 2, D_FF, D_MODEL), D_FF),
        "mem_w_kv": w(ks[7], (DEPTH, D_MODEL, 2 * MEM_WIDTH), D_MODEL),
        "a_w_in": w(ks[8], (na, D_MODEL, A_IN), D_MODEL),
        "a_q_norm": gain(ks[9], (na, MLA_Q_LORA)),
        "a_kv_norm": gain(ks[10], (na, MLA_KV_LORA)),
        "a_w_uq": w(ks[11], (na, MLA_Q_LORA, MLA_HEADS * (MLA_NOPE + MLA_ROPE)), MLA_Q_LORA),
        "a_w_ukv": w(ks[12], (na, MLA_KV_LORA, MLA_HEADS * (MLA_NOPE + MLA_V)), MLA_KV_LORA),
        "a_w_out": w(ks[13], (na, A_OUT, D_MODEL), A_OUT),
        "b_w_in": w(ks[14], (nb_, D_MODEL, B_IN), D_MODEL),
        "b_w_out": w(ks[15], (nb_, B_OUT, D_MODEL), B_OUT),
        "c_w_in": w(ks[16], (nc, D_MODEL, C_IN), D_MODEL),
        "c_conv_w": w(ks[17], (nc, CONV_WIDTH, CONV_CH), CONV_WIDTH),
        "c_conv_b": 0.02 * jax.random.normal(ks[18], (nc, CONV_CH), f32),
        "c_ln_g": gain(ks[19], (nc, CONV_CH)),
        "c_ln_b": 0.02 * jax.random.normal(ks[20], (nc, CONV_CH), f32),
        "c_w_out": w(ks[21], (nc, C_OUT, D_MODEL), C_OUT),
        "d_w_in": w(ks[22], (nd, D_MODEL, D_IN), D_MODEL),
        "d_conv_w": w(ks[23], (nd, SC_WIDTH, SC_CH), SC_WIDTH),
        "d_w_out": w(ks[24], (nd, D_OUT, D_MODEL), D_OUT),
    }


def reference(x, mem, positions, norm_g, ffn_w_gate, ffn_w_up, ffn_w_down, mem_w_kv,
              a_w_in, a_q_norm, a_kv_norm, a_w_uq, a_w_ukv, a_w_out,
              b_w_in, b_w_out,
              c_w_in, c_conv_w, c_conv_b, c_ln_g, c_ln_b, c_w_out,
              d_w_in, d_conv_w, d_w_out):
    for i in range(DEPTH):
        g = norm_g[i]
        h = rms_norm(x, g[0])
        x = x + 0.5 * rms_norm(swiglu(h, ffn_w_gate[i, 0], ffn_w_up[i, 0], ffn_w_down[i, 0]), g[1])
        h = rms_norm(x, g[2])
        m, j = i % N_MIXERS, i // N_MIXERS
        if m == 0:
            o, q_mem = mla_mixer(h, positions, a_w_in[j], a_q_norm[j], a_kv_norm[j], a_w_uq[j], a_w_ukv[j])
            w_out = a_w_out[j]
        elif m == 1:
            o, q_mem = dilated_mixer(h, b_w_in[j])
            w_out = b_w_out[j]
        elif m == 2:
            o, q_mem = conformer_conv_mixer(h, c_w_in[j], c_conv_w[j], c_conv_b[j], c_ln_g[j], c_ln_b[j])
            w_out = c_w_out[j]
        else:
            o, q_mem = short_conv_mixer(h, d_w_in[j], d_conv_w[j])
            w_out = d_w_out[j]
        mo = memory_attention(q_mem, rms_norm(mem, g[6]), mem_w_kv[i])
        y = jnp.concatenate([o, mo], axis=-1) @ w_out
        x = x + rms_norm(y, g[3])
        h = rms_norm(x, g[4])
        x = x + 0.5 * rms_norm(swiglu(h, ffn_w_gate[i, 1], ffn_w_up[i, 1], ffn_w_down[i, 1]), g[5])
    return x
```

```python
import functools
import math

import jax
import jax.numpy as jnp
from jax import lax
from jax.experimental import pallas as pl
from jax.experimental.pallas import tpu as pltpu

F32 = jnp.float32
BF = jnp.bfloat16

D_MODEL = 1024
SEQ = 16384
DEPTH = 4
N_MEM = 256
HEAD_DIM = 64
MEM_HEADS = 4
MEM_WIDTH = MEM_HEADS * HEAD_DIM
MLA_HEADS = 12
MLA_Q_LORA = 384
MLA_KV_LORA = 256
MLA_NOPE = 64
MLA_ROPE = 32
MLA_V = 64
ROPE_THETA = 10000.0
DIL_GROUPS = ((128, 1), (512, 4), (2048, 16))
DIL_HEADS = 8
DIL_W = DIL_HEADS * HEAD_DIM
ALIBI_MAX = 8.0
CONV_CH = 768
CONV_WIDTH = 31
SC_CH = 768
SC_WIDTH = 3
D_FF = 2816
EPS = 1e-6
NEG = -1e30

LANES = 128
TM = 512
TQ = 512
TK = 512
BAND_TB = 512
BAND_SB = 128
BAND_HALF = 64
CONV_ROWS = 32
VMEM_LIMIT = 56 << 20

NT_DIMS = (((1,), (1,)), ((), ()))


def _cp(n_axes):
    return pltpu.CompilerParams(dimension_semantics=("arbitrary",) * n_axes,
                                vmem_limit_bytes=VMEM_LIMIT)


def _full(shape):
    return pl.BlockSpec(shape, lambda *_: (0,) * len(shape), pipeline_mode=pl.Buffered(1))


def _rms(x, g):
    return x * lax.rsqrt(jnp.mean(x * x, axis=-1, keepdims=True) + EPS) * g


def _mem_attn(qm, mk_ref, mv_ref):
    kk = mk_ref[...]
    vv = mv_ref[...]
    lane = lax.broadcasted_iota(jnp.int32, qm.shape, 1)
    out = jnp.zeros_like(qm)
    for h in range(MEM_HEADS):
        msk = (lane >= h * HEAD_DIM) & (lane < (h + 1) * HEAD_DIM)
        qh = jnp.where(msk, qm, 0.0).astype(BF)
        s = lax.dot_general(qh, kk, NT_DIMS, preferred_element_type=F32) * (HEAD_DIM ** -0.5)
        m = jnp.max(s, axis=-1, keepdims=True)
        p = jnp.exp(s - m)
        l = jnp.sum(p, axis=-1, keepdims=True)
        o = jnp.dot(p.astype(BF), vv, preferred_element_type=F32) * (1.0 / l)
        out = jnp.where(msk, o, out)
    return out


def _ffn_kernel(x_ref, gi_ref, go_ref, wg_ref, wu_ref, wd_ref, o_ref):
    x = x_ref[...]
    h = _rms(x, gi_ref[...]).astype(BF)
    g = jnp.dot(h, wg_ref[...], preferred_element_type=F32)
    u = jnp.dot(h, wu_ref[...], preferred_element_type=F32)
    a = (g / (1.0 + jnp.exp(-g)) * u).astype(BF)
    y = jnp.dot(a, wd_ref[...], preferred_element_type=F32)
    o_ref[...] = x + 0.5 * _rms(y, go_ref[...])


def _ffn(x, gi, go, wg, wu, wd):
    row = pl.BlockSpec((TM, D_MODEL), lambda i: (i, 0))
    return pl.pallas_call(
        _ffn_kernel,
        grid=(SEQ // TM,),
        in_specs=[row, _full((1, D_MODEL)), _full((1, D_MODEL)),
                  _full((D_MODEL, D_FF)), _full((D_MODEL, D_FF)), _full((D_FF, D_MODEL))],
        out_specs=row,
        out_shape=jax.ShapeDtypeStruct((SEQ, D_MODEL), F32),
        compiler_params=_cp(1),
        name="ffn",
    )(x, gi, go, wg, wu, wd)


def _memkv_kernel(mem_ref, g_ref, w_ref, k_ref, v_ref):
    mn = _rms(mem_ref[...], g_ref[0]).astype(BF)
    kv = jnp.dot(mn, w_ref[0], preferred_element_type=F32)
    k_ref[0] = kv[:, :MEM_WIDTH].astype(BF)
    v_ref[0] = kv[:, MEM_WIDTH:].astype(BF)


def _memkv(mem, g6, w_kv):
    out = jax.ShapeDtypeStruct((DEPTH, N_MEM, MEM_WIDTH), BF)
    ospec = pl.BlockSpec((1, N_MEM, MEM_WIDTH), lambda i: (i, 0, 0))
    return pl.pallas_call(
        _memkv_kernel,
        grid=(DEPTH,),
        in_specs=[_full((N_MEM, D_MODEL)),
                  pl.BlockSpec((1, 1, D_MODEL), lambda i: (i, 0, 0)),
                  pl.BlockSpec((1, D_MODEL, 2 * MEM_WIDTH), lambda i: (i, 0, 0))],
        out_specs=[ospec, ospec],
        out_shape=[out, out],
        compiler_params=_cp(1),
        name="memkv",
    )(mem, g6, w_kv)


def _out_kernel(o_ref, mo_ref, x_ref, w_ref, g_ref, out_ref, *, wo):
    y = jnp.dot(o_ref[...], w_ref[:wo, :], preferred_element_type=F32)
    y = y + jnp.dot(mo_ref[...], w_ref[wo:, :], preferred_element_type=F32)
    out_ref[...] = x_ref[...] + _rms(y, g_ref[...])


def _out_proj(o, mo, x, w, g):
    wo = o.shape[1]
    row = pl.BlockSpec((TM, D_MODEL), lambda i: (i, 0))
    return pl.pallas_call(
        functools.partial(_out_kernel, wo=wo),
        grid=(SEQ // TM,),
        in_specs=[pl.BlockSpec((TM, wo), lambda i: (i, 0)),
                  pl.BlockSpec((TM, MEM_WIDTH), lambda i: (i, 0)),
                  row, _full(w.shape), _full((1, D_MODEL))],
        out_specs=row,
        out_shape=jax.ShapeDtypeStruct((SEQ, D_MODEL), F32),
        compiler_params=_cp(1),
        name="out_proj",
    )(o, mo, x, w, g)


def _rope(t, cos, sin_signed, lane):
    partner = jnp.where(lane < 80, pltpu.roll(t, LANES - 16, 1), pltpu.roll(t, 16, 1))
    return t * cos + partner * sin_signed


def _mla_in_kernel(x_ref, pos_ref, g_ref, win_ref, qg_ref, kvg_ref, wuq_ref, wuk_ref, wuvt_ref,
                   invf_ref, sgn_ref, mk_ref, mv_ref, q_ref, k_ref, vt_ref, mo_ref):
    h = _rms(x_ref[...], g_ref[...]).astype(BF)
    z = jnp.dot(h, win_ref[...], preferred_element_type=F32)
    o1, o2, o3 = MLA_Q_LORA, MLA_Q_LORA + MLA_KV_LORA, MLA_Q_LORA + MLA_KV_LORA + LANES
    cqn = _rms(z[:, :o1], qg_ref[...]).astype(BF)
    ckvn = _rms(z[:, o1:o2], kvg_ref[...]).astype(BF)
    krg = z[:, o2:o3]
    mo_ref[...] = _mem_attn(z[:, o3:], mk_ref, mv_ref).astype(BF)

    ang = pos_ref[...].astype(F32) * invf_ref[...]
    cos = jnp.cos(ang)
    sin_signed = jnp.sin(ang) * sgn_ref[...]
    lane = lax.broadcasted_iota(jnp.int32, (TM, LANES), 1)

    q = jnp.dot(cqn, wuq_ref[...], preferred_element_type=F32)
    kn = jnp.dot(ckvn, wuk_ref[...], preferred_element_type=F32)
    vt = lax.dot_general(wuvt_ref[...], ckvn, NT_DIMS, preferred_element_type=F32)
    kr = _rope(krg, cos, sin_signed, lane)
    ones_row = (lax.broadcasted_iota(jnp.int32, (LANES, TM), 0) == MLA_V).astype(F32)
    qscale = (MLA_NOPE + MLA_ROPE) ** -0.5 * math.log2(math.e)
    for hd in range(MLA_HEADS):
        sl = slice(hd * LANES, (hd + 1) * LANES)
        q_ref[hd] = (_rope(q[:, sl], cos, sin_signed, lane) * qscale).astype(BF)
        k_ref[hd] = (kn[:, sl] + kr).astype(BF)
        vt_ref[hd] = (vt[sl, :] + ones_row).astype(BF)


def _mla_in(x, pos, g, win, qg, kvg, wuq, wuk, wuvt, invf, sgn, mk, mv):
    hm = jax.ShapeDtypeStruct((MLA_HEADS, SEQ, LANES), BF)
    return pl.pallas_call(
        _mla_in_kernel,
        grid=(SEQ // TM,),
        in_specs=[pl.BlockSpec((TM, D_MODEL), lambda i: (i, 0)),
                  pl.BlockSpec((TM, 1), lambda i: (i, 0)),
                  _full((1, D_MODEL)), _full(win.shape), _full(qg.shape), _full(kvg.shape),
                  _full(wuq.shape), _full(wuk.shape), _full(wuvt.shape),
                  _full((1, LANES)), _full((1, LANES)), _full(mk.shape), _full(mv.shape)],
        out_specs=[pl.BlockSpec((MLA_HEADS, TM, LANES), lambda i: (0, i, 0)),
                   pl.BlockSpec((MLA_HEADS, TM, LANES), lambda i: (0, i, 0)),
                   pl.BlockSpec((MLA_HEADS, LANES, TM), lambda i: (0, 0, i)),
                   pl.BlockSpec((TM, MEM_WIDTH), lambda i: (i, 0))],
        out_shape=[hm, hm, jax.ShapeDtypeStruct((MLA_HEADS, LANES, SEQ), BF),
                   jax.ShapeDtypeStruct((SEQ, MEM_WIDTH), BF)],
        compiler_params=_cp(1),
        name="mla_in",
    )(x, pos, g, win, qg, kvg, wuq, wuk, wuvt, invf, sgn, mk, mv)


def _flash_kernel(q_ref, k_ref, vt_ref, o_ref, m_sc, acc_sc):
    outs = []
    for hh in range(2):
        q = q_ref[hh]
        m_sc[...] = jnp.full(m_sc.shape, NEG, F32)
        acc_sc[...] = jnp.zeros(acc_sc.shape, F32)

        def body(c, carry, hh=hh, q=q):
            c0 = pl.multiple_of(c * TK, TK)
            st = lax.dot_general(k_ref[hh, pl.ds(c0, TK), :], q, NT_DIMS,
                                 preferred_element_type=F32)
            m_prev = m_sc[...]
            m_new = jnp.maximum(m_prev, jnp.max(st, axis=0, keepdims=True))
            alpha = jnp.exp2(m_prev - m_new)
            pt = jnp.exp2(st - m_new).astype(BF)
            acc_sc[...] = alpha * acc_sc[...] + jnp.dot(vt_ref[hh, :, pl.ds(c0, TK)], pt,
                                                        preferred_element_type=F32)
            m_sc[...] = m_new
            return carry

        lax.fori_loop(0, SEQ // TK, body, 0)
        acc = acc_sc[...]
        outs.append(acc[:MLA_V] * (1.0 / acc[MLA_V:MLA_V + 1]))
    o_ref[...] = jnp.concatenate(outs, axis=0).T.astype(BF)


def _flash(q, k, vt):
    return pl.pallas_call(
        _flash_kernel,
        grid=(MLA_HEADS // 2, SEQ // TQ),
        in_specs=[pl.BlockSpec((2, TQ, LANES), lambda p, i: (p, i, 0)),
                  pl.BlockSpec((2, SEQ, LANES), lambda p, i: (p, 0, 0), pipeline_mode=pl.Buffered(1)),
                  pl.BlockSpec((2, LANES, SEQ), lambda p, i: (p, 0, 0), pipeline_mode=pl.Buffered(1))],
        out_specs=pl.BlockSpec((TQ, LANES), lambda p, i: (i, p)),
        out_shape=jax.ShapeDtypeStruct((SEQ, MLA_HEADS * MLA_V), BF),
        scratch_shapes=[pltpu.VMEM((1, TQ), F32), pltpu.VMEM((LANES, TQ), F32)],
        compiler_params=_cp(2),
        name="mla_flash",
    )(q, k, vt)


def _mem_in_kernel(x_ref, g_ref, w_ref, mk_ref, mv_ref, mo_ref):
    h = _rms(x_ref[...], g_ref[...]).astype(BF)
    qm = jnp.dot(h, w_ref[...], preferred_element_type=F32)
    mo_ref[...] = _mem_attn(qm, mk_ref, mv_ref).astype(BF)


def _mem_in(x, g, w, mk, mv):
    return pl.pallas_call(
        _mem_in_kernel,
        grid=(SEQ // TM,),
        in_specs=[pl.BlockSpec((TM, D_MODEL), lambda i: (i, 0)), _full((1, D_MODEL)),
                  _full(w.shape), _full(mk.shape), _full(mv.shape)],
        out_specs=pl.BlockSpec((TM, MEM_WIDTH), lambda i: (i, 0)),
        out_shape=jax.ShapeDtypeStruct((SEQ, MEM_WIDTH), BF),
        compiler_params=_cp(1),
        name="mem_in",
    )(x, g, w, mk, mv)


def _proj_kernel(x_ref, g_ref, w_ref, o_ref):
    h = _rms(x_ref[...], g_ref[...]).astype(BF)
    o_ref[...] = jnp.dot(h, w_ref[...], preferred_element_type=F32).astype(BF)


def _dil_proj(x, g, w, d):
    sub = SEQ // d
    nb = sub // TM
    n = w.shape[1]
    return pl.pallas_call(
        _proj_kernel,
        grid=(d, nb),
        in_specs=[pl.BlockSpec((TM, D_MODEL), lambda r, j: (j, r)), _full((1, D_MODEL)), _full(w.shape)],
        out_specs=pl.BlockSpec((TM, n), lambda r, j: (r * nb + j, 0)),
        out_shape=jax.ShapeDtypeStruct((SEQ, n), BF),
        compiler_params=_cp(2),
        name=f"dil_proj_d{d}",
    )(x.reshape(sub, d * D_MODEL), g, w)


def _band_kernel(q_ref, k_ref, v_ref, kp_ref, vp_ref, kn_ref, vn_ref, o_ref, lse_ref, kbuf, vbuf,
                 *, dil, sub_len, slopes):
    j = pl.program_id(1)
    hb = BAND_HALF
    kbuf[0:hb] = kp_ref[...]
    kbuf[hb:hb + BAND_TB] = k_ref[...]
    kbuf[hb + BAND_TB:] = kn_ref[...]
    vbuf[0:hb] = vp_ref[...]
    vbuf[hb:hb + BAND_TB] = v_ref[...]
    vbuf[hb + BAND_TB:] = vn_ref[...]
    win = BAND_SB + 2 * hb
    qi = lax.broadcasted_iota(jnp.int32, (BAND_SB, win), 0)
    kj = lax.broadcasted_iota(jnp.int32, (BAND_SB, win), 1)
    rel = jnp.abs(kj - hb - qi)
    dist = rel.astype(F32) * float(dil)
    lane = lax.broadcasted_iota(jnp.int32, (BAND_SB, LANES), 1)
    for sb in range(BAND_TB // BAND_SB):
        r0 = sb * BAND_SB
        key_idx = j * BAND_TB + (r0 - hb) + kj
        valid = (rel <= hb) & (key_idx >= 0) & (key_idx < sub_len)
        for hp in range(DIL_W // LANES):
            cs = slice(hp * LANES, (hp + 1) * LANES)
            q2 = q_ref[r0:r0 + BAND_SB, cs].astype(F32)
            k2 = kbuf[r0:r0 + win, cs]
            v2 = vbuf[r0:r0 + win, cs]
            o_pair = jnp.zeros((BAND_SB, LANES), F32)
            l_pair = jnp.zeros((BAND_SB, LANES), F32)
            for hh in range(2):
                lm = (lane >= hh * HEAD_DIM) & (lane < (hh + 1) * HEAD_DIM)
                qh = jnp.where(lm, q2, 0.0).astype(BF)
                s = lax.dot_general(qh, k2, NT_DIMS, preferred_element_type=F32) * (HEAD_DIM ** -0.5)
                s = s - slopes[2 * hp + hh] * dist
                s = jnp.where(valid, s, NEG)
                m = jnp.max(s, axis=-1, keepdims=True)
                p = jnp.exp(s - m)
                l = jnp.sum(p, axis=-1, keepdims=True)
                o = jnp.dot(p.astype(BF), v2, preferred_element_type=F32) * (1.0 / l)
                o_pair = jnp.where(lm, o, o_pair)
                l_pair = jnp.where(lm, m + jnp.log(l), l_pair)
            o_ref[r0:r0 + BAND_SB, cs] = o_pair.astype(BF)
            lse_ref[r0:r0 + BAND_SB, cs] = l_pair


def _band_attn(qkv, g_idx, d):
    sub = SEQ // d
    nb = sub // BAND_TB
    hpb = BAND_TB // BAND_HALF
    n_halo = SEQ // BAND_HALF
    n_heads = len(DIL_GROUPS) * DIL_HEADS
    slopes = tuple(2.0 ** (-ALIBI_MAX * (g_idx * DIL_HEADS + h + 1.0) / n_heads) for h in range(DIL_HEADS))

    def main(col):
        return pl.BlockSpec((BAND_TB, DIL_W), lambda r, j: (r * nb + j, col))

    def prev(col):
        return pl.BlockSpec((BAND_HALF, DIL_W), lambda r, j: (jnp.maximum((r * nb + j) * hpb - 1, 0), col))

    def nxt(col):
        return pl.BlockSpec((BAND_HALF, DIL_W),
                            lambda r, j: (jnp.minimum((r * nb + j + 1) * hpb, n_halo - 1), col))

    ospec = pl.BlockSpec((BAND_TB, DIL_W), lambda r, j: (j, r))
    o, lse = pl.pallas_call(
        functools.partial(_band_kernel, dil=d, sub_len=sub, slopes=slopes),
        grid=(d, nb),
        in_specs=[main(0), main(1), main(2), prev(1), prev(2), nxt(1), nxt(2)],
        out_specs=[ospec, ospec],
        out_shape=[jax.ShapeDtypeStruct((sub, d * DIL_W), BF), jax.ShapeDtypeStruct((sub, d * DIL_W), F32)],
        scratch_shapes=[pltpu.VMEM((BAND_TB + 2 * BAND_HALF, DIL_W), BF),
                        pltpu.VMEM((BAND_TB + 2 * BAND_HALF, DIL_W), BF)],
        compiler_params=_cp(2),
        name=f"band_attn_d{d}",
    )(qkv, qkv, qkv, qkv, qkv, qkv, qkv)
    return o.reshape(SEQ, DIL_W), lse.reshape(SEQ, DIL_W)


def _dil_out_kernel(o0, o1, o2, l0, l1, l2, mo_ref, x_ref, w_ref, g_ref, out_ref):
    a0, a1, a2 = l0[...], l1[...], l2[...]
    mx = jnp.maximum(jnp.maximum(a0, a1), a2)
    e0, e1, e2 = jnp.exp(a0 - mx), jnp.exp(a1 - mx), jnp.exp(a2 - mx)
    inv = 1.0 / (e0 + e1 + e2)
    o = (e0 * inv) * o0[...].astype(F32) + (e1 * inv) * o1[...].astype(F32) + (e2 * inv) * o2[...].astype(F32)
    y = jnp.dot(o.astype(BF), w_ref[:DIL_W, :], preferred_element_type=F32)
    y = y + jnp.dot(mo_ref[...], w_ref[DIL_W:, :], preferred_element_type=F32)
    out_ref[...] = x_ref[...] + _rms(y, g_ref[...])


def _dil_out(os_, ls_, mo, x, w, g):
    row = pl.BlockSpec((TM, D_MODEL), lambda i: (i, 0))
    half = pl.BlockSpec((TM, DIL_W), lambda i: (i, 0))
    return pl.pallas_call(
        _dil_out_kernel,
        grid=(SEQ // TM,),
        in_specs=[half] * 6 + [pl.BlockSpec((TM, MEM_WIDTH), lambda i: (i, 0)), row,
                               _full(w.shape), _full((1, D_MODEL))],
        out_specs=row,
        out_shape=jax.ShapeDtypeStruct((SEQ, D_MODEL), F32),
        compiler_params=_cp(1),
        name="dil_out",
    )(*os_, *ls_, mo, x, w, g)


def _conf_in_kernel(x_ref, g_ref, w_ref, mk_ref, mv_ref, u_ref, mo_ref):
    h = _rms(x_ref[...], g_ref[...]).astype(BF)
    z = jnp.dot(h, w_ref[...], preferred_element_type=F32)
    a, gate = z[:, :CONV_CH], z[:, CONV_CH:2 * CONV_CH]
    u_ref[...] = a / (1.0 + jnp.exp(-gate))
    mo_ref[...] = _mem_attn(z[:, 2 * CONV_CH:], mk_ref, mv_ref).astype(BF)


def _conf_in(x, g, w, mk, mv):
    return pl.pallas_call(
        _conf_in_kernel,
        grid=(SEQ // TM,),
        in_specs=[pl.BlockSpec((TM, D_MODEL), lambda i: (i, 0)), _full((1, D_MODEL)),
                  _full(w.shape), _full(mk.shape), _full(mv.shape)],
        out_specs=[pl.BlockSpec((TM, CONV_CH), lambda i: (i, 0)),
                   pl.BlockSpec((TM, MEM_WIDTH), lambda i: (i, 0))],
        out_shape=[jax.ShapeDtypeStruct((SEQ, CONV_CH), F32), jax.ShapeDtypeStruct((SEQ, MEM_WIDTH), BF)],
        compiler_params=_cp(1),
        name="conf_in",
    )(x, g, w, mk, mv)


def _sc_in_kernel(x_ref, g_ref, w_ref, mk_ref, mv_ref, bg_ref, v_ref, mo_ref):
    h = _rms(x_ref[...], g_ref[...]).astype(BF)
    z = jnp.dot(h, w_ref[...], preferred_element_type=F32)
    bg_ref[...] = z[:, :SC_CH]
    v_ref[...] = z[:, SC_CH:2 * SC_CH] * z[:, 2 * SC_CH:3 * SC_CH]
    mo_ref[...] = _mem_attn(z[:, 3 * SC_CH:], mk_ref, mv_ref).astype(BF)


def _sc_in(x, g, w, mk, mv):
    ch = pl.BlockSpec((TM, SC_CH), lambda i: (i, 0))
    chs = jax.ShapeDtypeStruct((SEQ, SC_CH), F32)
    return pl.pallas_call(
        _sc_in_kernel,
        grid=(SEQ // TM,),
        in_specs=[pl.BlockSpec((TM, D_MODEL), lambda i: (i, 0)), _full((1, D_MODEL)),
                  _full(w.shape), _full(mk.shape), _full(mv.shape)],
        out_specs=[ch, ch, pl.BlockSpec((TM, MEM_WIDTH), lambda i: (i, 0))],
        out_shape=[chs, chs, jax.ShapeDtypeStruct((SEQ, MEM_WIDTH), BF)],
        compiler_params=_cp(1),
        name="sc_in",
    )(x, g, w, mk, mv)


def _fill_window(win, u_ref, up_ref, un_ref, halo):
    i = pl.program_id(0)
    win[0:halo] = jnp.where(i > 0, up_ref[...], 0.0)
    win[halo:halo + TM] = u_ref[...]
    win[halo + TM:] = jnp.where(i < pl.num_programs(0) - 1, un_ref[...], 0.0)


def _dwconv_chunk(win, cw_ref, r0, width, halo):
    base = r0 + halo - width // 2
    acc = cw_ref[0:1, :] * win[base:base + CONV_ROWS, :]
    for k in range(1, width):
        acc = acc + cw_ref[k:k + 1, :] * win[base + k:base + k + CONV_ROWS, :]
    return acc


def _conf_out_kernel(u_ref, up_ref, un_ref, cw_ref, cb_ref, lg_ref, lb_ref, mo_ref, x_ref, w_ref, g_ref,
                     out_ref, win, o_sc, *, halo):
    _fill_window(win, u_ref, up_ref, un_ref, halo)
    for c in range(TM // CONV_ROWS):
        r0 = c * CONV_ROWS
        t = _dwconv_chunk(win, cw_ref, r0, CONV_WIDTH, halo) + cb_ref[...]
        mu = jnp.mean(t, axis=-1, keepdims=True)
        tc = t - mu
        var = jnp.mean(tc * tc, axis=-1, keepdims=True)
        t = tc * lax.rsqrt(var + EPS) * lg_ref[...] + lb_ref[...]
        o_sc[r0:r0 + CONV_ROWS, :] = (t / (1.0 + jnp.exp(-t))).astype(BF)
    y = jnp.dot(o_sc[...], w_ref[:CONV_CH, :], preferred_element_type=F32)
    y = y + jnp.dot(mo_ref[...], w_ref[CONV_CH:, :], preferred_element_type=F32)
    out_ref[...] = x_ref[...] + _rms(y, g_ref[...])


def _sc_out_kernel(v_ref, vp_ref, vn_ref, cw_ref, bg_ref, mo_ref, x_ref, w_ref, g_ref,
                   out_ref, win, o_sc, *, halo):
    _fill_window(win, v_ref, vp_ref, vn_ref, halo)
    for c in range(TM // CONV_ROWS):
        r0 = c * CONV_ROWS
        t = _dwconv_chunk(win, cw_ref, r0, SC_WIDTH, halo)
        o_sc[r0:r0 + CONV_ROWS, :] = (bg_ref[r0:r0 + CONV_ROWS, :] * t).astype(BF)
    y = jnp.dot(o_sc[...], w_ref[:SC_CH, :], preferred_element_type=F32)
    y = y + jnp.dot(mo_ref[...], w_ref[SC_CH:, :], preferred_element_type=F32)
    out_ref[...] = x_ref[...] + _rms(y, g_ref[...])


def _conv_specs(ch, halo):
    hpb = TM // halo
    n_halo = SEQ // halo
    return [pl.BlockSpec((TM, ch), lambda i: (i, 0)),
            pl.BlockSpec((halo, ch), lambda i: (jnp.maximum(i * hpb - 1, 0), 0)),
            pl.BlockSpec((halo, ch), lambda i: (jnp.minimum((i + 1) * hpb, n_halo - 1), 0))]


def _conf_out(u, cw, cb, lg, lb, mo, x, w, g):
    halo = 16
    row = pl.BlockSpec((TM, D_MODEL), lambda i: (i, 0))
    return pl.pallas_call(
        functools.partial(_conf_out_kernel, halo=halo),
        grid=(SEQ // TM,),
        in_specs=_conv_specs(CONV_CH, halo) + [
            _full(cw.shape), _full(cb.shape), _full(lg.shape), _full(lb.shape),
            pl.BlockSpec((TM, MEM_WIDTH), lambda i: (i, 0)), row, _full(w.shape), _full((1, D_MODEL))],
        out_specs=row,
        out_shape=jax.ShapeDtypeStruct((SEQ, D_MODEL), F32),
        scratch_shapes=[pltpu.VMEM((TM + 2 * halo, CONV_CH), F32), pltpu.VMEM((TM, CONV_CH), BF)],
        compiler_params=_cp(1),
        name="conf_out",
    )(u, u, u, cw, cb, lg, lb, mo, x, w, g)


def _sc_out(v, cw, bg, mo, x, w, g):
    halo = 8
    row = pl.BlockSpec((TM, D_MODEL), lambda i: (i, 0))
    return pl.pallas_call(
        functools.partial(_sc_out_kernel, halo=halo),
        grid=(SEQ // TM,),
        in_specs=_conv_specs(SC_CH, halo) + [
            _full(cw.shape), pl.BlockSpec((TM, SC_CH), lambda i: (i, 0)),
            pl.BlockSpec((TM, MEM_WIDTH), lambda i: (i, 0)), row, _full(w.shape), _full((1, D_MODEL))],
        out_specs=row,
        out_shape=jax.ShapeDtypeStruct((SEQ, D_MODEL), F32),
        scratch_shapes=[pltpu.VMEM((TM + 2 * halo, SC_CH), F32), pltpu.VMEM((TM, SC_CH), BF)],
        compiler_params=_cp(1),
        name="sc_out",
    )(v, v, v, cw, bg, mo, x, w, g)


def _mla_weights(w_in, w_uq, w_ukv):
    o1 = MLA_Q_LORA
    o2 = o1 + MLA_KV_LORA
    o3 = o2 + MLA_ROPE
    zeros = lambda n: jnp.zeros((D_MODEL, n), w_in.dtype)
    win = jnp.concatenate([w_in[:, :o2], zeros(MLA_V), w_in[:, o2:o3], zeros(LANES - MLA_V - MLA_ROPE),
                           w_in[:, o3:]], axis=1).astype(BF)
    qd = MLA_NOPE + MLA_ROPE
    wuq = jnp.pad(w_uq.reshape(MLA_Q_LORA, MLA_HEADS, qd), ((0, 0), (0, 0), (0, LANES - qd)))
    wuq = wuq.reshape(MLA_Q_LORA, MLA_HEADS * LANES).astype(BF)
    wkv = w_ukv.reshape(MLA_KV_LORA, MLA_HEADS, MLA_NOPE + MLA_V)
    pad = ((0, 0), (0, 0), (0, LANES - MLA_NOPE))
    wuk = jnp.pad(wkv[..., :MLA_NOPE], pad).reshape(MLA_KV_LORA, MLA_HEADS * LANES).astype(BF)
    wuv = jnp.pad(wkv[..., MLA_NOPE:], pad).reshape(MLA_KV_LORA, MLA_HEADS * LANES)
    return win, wuq, wuk, wuv.T.astype(BF)


def _rope_tables():
    half = MLA_ROPE // 2
    inv = ROPE_THETA ** (-jnp.arange(half, dtype=F32) / half)
    z = lambda n: jnp.zeros((n,), F32)
    invf = jnp.concatenate([z(MLA_NOPE), inv, inv, z(LANES - MLA_NOPE - MLA_ROPE)])
    sgn = jnp.concatenate([z(MLA_NOPE), -jnp.ones((half,), F32), jnp.ones((half,), F32),
                           z(LANES - MLA_NOPE - MLA_ROPE)])
    return invf[None, :], sgn[None, :]


def kernel(x, mem, positions, norm_g, ffn_w_gate, ffn_w_up, ffn_w_down, mem_w_kv, a_w_in, a_q_norm, a_kv_norm, a_w_uq, a_w_ukv, a_w_out, b_w_in, b_w_out, c_w_in, c_conv_w, c_conv_b, c_ln_g, c_ln_b, c_w_out, d_w_in, d_conv_w, d_w_out):
    assert x.shape == (1, SEQ, D_MODEL) and mem.shape == (1, N_MEM, D_MODEL)
    xs = x[0]
    pos = positions.reshape(SEQ, 1)
    gn = lambda i, j: norm_g[i, j][None, :]
    mem_k, mem_v = _memkv(mem[0], norm_g[:, 6][:, None, :], mem_w_kv.astype(BF))
    invf, sgn = _rope_tables()
    n_mix = 4
    for i in range(DEPTH):
        xs = _ffn(xs, gn(i, 0), gn(i, 1), ffn_w_gate[i, 0].astype(BF), ffn_w_up[i, 0].astype(BF),
                  ffn_w_down[i, 0].astype(BF))
        m, j = i % n_mix, i // n_mix
        mk, mv = mem_k[i], mem_v[i]
        if m == 0:
            win, wuq, wuk, wuvt = _mla_weights(a_w_in[j], a_w_uq[j], a_w_ukv[j])
            q, k, vt, mo = _mla_in(xs, pos, gn(i, 2), win, a_q_norm[j][None, :], a_kv_norm[j][None, :],
                                   wuq, wuk, wuvt, invf, sgn, mk, mv)
            o = _flash(q, k, vt)
            xs = _out_proj(o, mo, xs, a_w_out[j].astype(BF), gn(i, 3))
        elif m == 1:
            w = b_w_in[j].astype(BF)
            gw = 3 * DIL_W
            mo = _mem_in(xs, gn(i, 2), w[:, len(DIL_GROUPS) * gw:], mk, mv)
            os_, ls_ = [], []
            for gi, (_, d) in enumerate(DIL_GROUPS):
                qkv = _dil_proj(xs, gn(i, 2), w[:, gi * gw:(gi + 1) * gw], d)
                o, lse = _band_attn(qkv, gi, d)
                os_.append(o)
                ls_.append(lse)
            xs = _dil_out(os_, ls_, mo, xs, b_w_out[j].astype(BF), gn(i, 3))
        elif m == 2:
            u, mo = _conf_in(xs, gn(i, 2), c_w_in[j].astype(BF), mk, mv)
            xs = _conf_out(u, c_conv_w[j], c_conv_b[j][None, :], c_ln_g[j][None, :], c_ln_b[j][None, :],
                           mo, xs, c_w_out[j].astype(BF), gn(i, 3))
        else:
            bg, v, mo = _sc_in(xs, gn(i, 2), d_w_in[j].astype(BF), mk, mv)
            xs = _sc_out(v, d_conv_w[j], bg, mo, xs, d_w_out[j].astype(BF), gn(i, 3))
        xs = _ffn(xs, gn(i, 4), gn(i, 5), ffn_w_gate[i, 1].astype(BF), ffn_w_up[i, 1].astype(BF),
                  ffn_w_down[i, 1].astype(BF))
    return xs[None]
```

```python
import functools
import math

import jax
import jax.numpy as jnp
from jax import lax
from jax.experimental import pallas as pl
from jax.experimental.pallas import tpu as pltpu

F32 = jnp.float32
BF = jnp.bfloat16

D_MODEL = 1024
SEQ = 16384
DEPTH = 4
N_MEM = 256
HEAD_DIM = 64
MEM_HEADS = 4
MEM_WIDTH = MEM_HEADS * HEAD_DIM
MLA_HEADS = 12
MLA_Q_LORA = 384
MLA_KV_LORA = 256
MLA_NOPE = 64
MLA_ROPE = 32
MLA_V = 64
ROPE_THETA = 10000.0
DIL_GROUPS = ((128, 1), (512, 4), (2048, 16))
DIL_HEADS = 8
DIL_W = DIL_HEADS * HEAD_DIM
ALIBI_MAX = 8.0
CONV_CH = 768
CONV_WIDTH = 31
SC_CH = 768
SC_WIDTH = 3
D_FF = 2816
EPS = 1e-6
NEG = -1e30

LANES = 128
TM = 512
TQ = 512
TK = 512
BAND_TB = 512
BAND_SB = 128
BAND_HALF = 64
CONV_ROWS = 32
VMEM_LIMIT = 56 << 20

NT_DIMS = (((1,), (1,)), ((), ()))


def _cp(n_axes):
    return pltpu.CompilerParams(dimension_semantics=("arbitrary",) * n_axes,
                                vmem_limit_bytes=VMEM_LIMIT)


def _full(shape):
    return pl.BlockSpec(shape, lambda *_: (0,) * len(shape), pipeline_mode=pl.Buffered(1))


def _rms(x, g):
    return x * lax.rsqrt(jnp.mean(x * x, axis=-1, keepdims=True) + EPS) * g


def _mem_attn(qm, mk_ref, mv_ref):
    kk = mk_ref[...]
    vv = mv_ref[...]
    lane = lax.broadcasted_iota(jnp.int32, qm.shape, 1)
    out = jnp.zeros_like(qm)
    for h in range(MEM_HEADS):
        msk = (lane >= h * HEAD_DIM) & (lane < (h + 1) * HEAD_DIM)
        qh = jnp.where(msk, qm, 0.0).astype(BF)
        s = lax.dot_general(qh, kk, NT_DIMS, preferred_element_type=F32) * (HEAD_DIM ** -0.5)
        m = jnp.max(s, axis=-1, keepdims=True)
        p = jnp.exp(s - m)
        l = jnp.sum(p, axis=-1, keepdims=True)
        o = jnp.dot(p.astype(BF), vv, preferred_element_type=F32) * (1.0 / l)
        out = jnp.where(msk, o, out)
    return out


def _ffn_kernel(x_ref, gi_ref, go_ref, wg_ref, wu_ref, wd_ref, o_ref):
    x = x_ref[...]
    h = _rms(x, gi_ref[...]).astype(BF)
    g = jnp.dot(h, wg_ref[...], preferred_element_type=F32)
    u = jnp.dot(h, wu_ref[...], preferred_element_type=F32)
    a = (g / (1.0 + jnp.exp(-g)) * u).astype(BF)
    y = jnp.dot(a, wd_ref[...], preferred_element_type=F32)
    o_ref[...] = x + 0.5 * _rms(y, go_ref[...])


def _ffn(x, gi, go, wg, wu, wd):
    row = pl.BlockSpec((TM, D_MODEL), lambda i: (i, 0))
    return pl.pallas_call(
        _ffn_kernel,
        grid=(SEQ // TM,),
        in_specs=[row, _full((1, D_MODEL)), _full((1, D_MODEL)),
                  _full((D_MODEL, D_FF)), _full((D_MODEL, D_FF)), _full((D_FF, D_MODEL))],
        out_specs=row,
        out_shape=jax.ShapeDtypeStruct((SEQ, D_MODEL), F32),
        compiler_params=_cp(1),
        name="ffn",
    )(x, gi, go, wg, wu, wd)


def _memkv_kernel(mem_ref, g_ref, w_ref, k_ref, v_ref):
    mn = _rms(mem_ref[...], g_ref[0]).astype(BF)
    kv = jnp.dot(mn, w_ref[0], preferred_element_type=F32)
    k_ref[0] = kv[:, :MEM_WIDTH].astype(BF)
    v_ref[0] = kv[:, MEM_WIDTH:].astype(BF)


def _memkv(mem, g6, w_kv):
    out = jax.ShapeDtypeStruct((DEPTH, N_MEM, MEM_WIDTH), BF)
    ospec = pl.BlockSpec((1, N_MEM, MEM_WIDTH), lambda i: (i, 0, 0))
    return pl.pallas_call(
        _memkv_kernel,
        grid=(DEPTH,),
        in_specs=[_full((N_MEM, D_MODEL)),
                  pl.BlockSpec((1, 1, D_MODEL), lambda i: (i, 0, 0)),
                  pl.BlockSpec((1, D_MODEL, 2 * MEM_WIDTH), lambda i: (i, 0, 0))],
        out_specs=[ospec, ospec],
        out_shape=[out, out],
        compiler_params=_cp(1),
        name="memkv",
    )(mem, g6, w_kv)


def _out_kernel(o_ref, mo_ref, x_ref, w_ref, g_ref, out_ref, *, wo):
    y = jnp.dot(o_ref[...], w_ref[:wo, :], preferred_element_type=F32)
    y = y + jnp.dot(mo_ref[...], w_ref[wo:, :], preferred_element_type=F32)
    out_ref[...] = x_ref[...] + _rms(y, g_ref[...])


def _out_proj(o, mo, x, w, g):
    wo = o.shape[1]
    row = pl.BlockSpec((TM, D_MODEL), lambda i: (i, 0))
    return pl.pallas_call(
        functools.partial(_out_kernel, wo=wo),
        grid=(SEQ // TM,),
        in_specs=[pl.BlockSpec((TM, wo), lambda i: (i, 0)),
                  pl.BlockSpec((TM, MEM_WIDTH), lambda i: (i, 0)),
                  row, _full(w.shape), _full((1, D_MODEL))],
        out_specs=row,
        out_shape=jax.ShapeDtypeStruct((SEQ, D_MODEL), F32),
        compiler_params=_cp(1),
        name="out_proj",
    )(o, mo, x, w, g)


def _rope(t, cos, sin_signed, lane):
    partner = jnp.where(lane < 80, pltpu.roll(t, LANES - 16, 1), pltpu.roll(t, 16, 1))
    return t * cos + partner * sin_signed


def _mla_in_kernel(x_ref, pos_ref, g_ref, win_ref, qg_ref, kvg_ref, wuq_ref, wuk_ref, wuvt_ref,
                   invf_ref, sgn_ref, mk_ref, mv_ref, q_ref, k_ref, vt_ref, mo_ref):
    h = _rms(x_ref[...], g_ref[...]).astype(BF)
    z = jnp.dot(h, win_ref[...], preferred_element_type=F32)
    o1, o2, o3 = MLA_Q_LORA, MLA_Q_LORA + MLA_KV_LORA, MLA_Q_LORA + MLA_KV_LORA + LANES
    cqn = _rms(z[:, :o1], qg_ref[...]).astype(BF)
    ckvn = _rms(z[:, o1:o2], kvg_ref[...]).astype(BF)
    krg = z[:, o2:o3]
    mo_ref[...] = _mem_attn(z[:, o3:], mk_ref, mv_ref).astype(BF)

    ang = pos_ref[...].astype(F32) * invf_ref[...]
    cos = jnp.cos(ang)
    sin_signed = jnp.sin(ang) * sgn_ref[...]
    lane = lax.broadcasted_iota(jnp.int32, (TM, LANES), 1)

    q = jnp.dot(cqn, wuq_ref[...], preferred_element_type=F32)
    kn = jnp.dot(ckvn, wuk_ref[...], preferred_element_type=F32)
    vt = lax.dot_general(wuvt_ref[...], ckvn, NT_DIMS, preferred_element_type=F32)
    kr = _rope(krg, cos, sin_signed, lane)
    ones_row = (lax.broadcasted_iota(jnp.int32, (LANES, TM), 0) == MLA_V).astype(F32)
    qscale = (MLA_NOPE + MLA_ROPE) ** -0.5 * math.log2(math.e)
    for hd in range(MLA_HEADS):
        sl = slice(hd * LANES, (hd + 1) * LANES)
        q_ref[hd] = (_rope(q[:, sl], cos, sin_signed, lane) * qscale).astype(BF)
        k_ref[hd] = (kn[:, sl] + kr).astype(BF)
        vt_ref[hd] = (vt[sl, :] + ones_row).astype(BF)


def _mla_in(x, pos, g, win, qg, kvg, wuq, wuk, wuvt, invf, sgn, mk, mv):
    hm = jax.ShapeDtypeStruct((MLA_HEADS, SEQ, LANES), BF)
    return pl.pallas_call(
        _mla_in_kernel,
        grid=(SEQ // TM,),
        in_specs=[pl.BlockSpec((TM, D_MODEL), lambda i: (i, 0)),
                  pl.BlockSpec((TM, 1), lambda i: (i, 0)),
                  _full((1, D_MODEL)), _full(win.shape), _full(qg.shape), _full(kvg.shape),
                  _full(wuq.shape), _full(wuk.shape), _full(wuvt.shape),
                  _full((1, LANES)), _full((1, LANES)), _full(mk.shape), _full(mv.shape)],
        out_specs=[pl.BlockSpec((MLA_HEADS, TM, LANES), lambda i: (0, i, 0)),
                   pl.BlockSpec((MLA_HEADS, TM, LANES), lambda i: (0, i, 0)),
                   pl.BlockSpec((MLA_HEADS, LANES, TM), lambda i: (0, 0, i)),
                   pl.BlockSpec((TM, MEM_WIDTH), lambda i: (i, 0))],
        out_shape=[hm, hm, jax.ShapeDtypeStruct((MLA_HEADS, LANES, SEQ), BF),
                   jax.ShapeDtypeStruct((SEQ, MEM_WIDTH), BF)],
        compiler_params=_cp(1),
        name="mla_in",
    )(x, pos, g, win, qg, kvg, wuq, wuk, wuvt, invf, sgn, mk, mv)


def _flash_kernel(q_ref, k_ref, vt_ref, o_ref, s00, s01, s10, s11, c00, c01, c10, c11, m0, m1, a0, a1):
    s_sc = ((s00, s01), (s10, s11))
    cm_sc = ((c00, c01), (c10, c11))
    m_sc = (m0, m1)
    acc_sc = (a0, a1)
    n_chunks = SEQ // TK

    def scores(c, slot):
        c0 = pl.multiple_of(c * TK, TK)
        for hh in range(2):
            st = lax.dot_general(k_ref[hh, pl.ds(c0, TK), :], q_ref[hh], NT_DIMS,
                                 preferred_element_type=F32)
            s_sc[hh][slot][...] = st
            cm_sc[hh][slot][...] = jnp.max(st, axis=0, keepdims=True)

    def accumulate(c, slot):
        c0 = pl.multiple_of(c * TK, TK)
        for hh in range(2):
            m_prev = m_sc[hh][...]
            m_new = jnp.maximum(m_prev, cm_sc[hh][slot][...])
            alpha = jnp.exp2(m_prev - m_new)
            pt = jnp.exp2(s_sc[hh][slot][...] - m_new).astype(BF)
            acc_sc[hh][...] = alpha * acc_sc[hh][...] + jnp.dot(vt_ref[hh, :, pl.ds(c0, TK)], pt,
                                                                preferred_element_type=F32)
            m_sc[hh][...] = m_new

    for hh in range(2):
        m_sc[hh][...] = jnp.full(m_sc[hh].shape, NEG, F32)
        acc_sc[hh][...] = jnp.zeros(acc_sc[hh].shape, F32)
    scores(0, 0)

    def body(c2, carry):
        c = 2 * c2
        scores(c + 1, 1)
        accumulate(c, 0)
        scores(jnp.minimum(c + 2, n_chunks - 1), 0)
        accumulate(c + 1, 1)
        return carry

    lax.fori_loop(0, n_chunks // 2, body, 0)
    outs = []
    for hh in range(2):
        acc = acc_sc[hh][...]
        outs.append(acc[:MLA_V] * (1.0 / acc[MLA_V:MLA_V + 1]))
    o_ref[...] = jnp.concatenate(outs, axis=0).T.astype(BF)


def _flash(q, k, vt):
    return pl.pallas_call(
        _flash_kernel,
        grid=(MLA_HEADS // 2, SEQ // TQ),
        in_specs=[pl.BlockSpec((2, TQ, LANES), lambda p, i: (p, i, 0)),
                  pl.BlockSpec((2, SEQ, LANES), lambda p, i: (p, 0, 0), pipeline_mode=pl.Buffered(1)),
                  pl.BlockSpec((2, LANES, SEQ), lambda p, i: (p, 0, 0), pipeline_mode=pl.Buffered(1))],
        out_specs=pl.BlockSpec((TQ, LANES), lambda p, i: (i, p)),
        out_shape=jax.ShapeDtypeStruct((SEQ, MLA_HEADS * MLA_V), BF),
        scratch_shapes=([pltpu.VMEM((TK, TQ), F32)] * 4 + [pltpu.VMEM((1, TQ), F32)] * 6
                        + [pltpu.VMEM((LANES, TQ), F32)] * 2),
        compiler_params=_cp(2),
        name="mla_flash",
    )(q, k, vt)


def _mem_in_kernel(x_ref, g_ref, w_ref, mk_ref, mv_ref, mo_ref):
    h = _rms(x_ref[...], g_ref[...]).astype(BF)
    qm = jnp.dot(h, w_ref[...], preferred_element_type=F32)
    mo_ref[...] = _mem_attn(qm, mk_ref, mv_ref).astype(BF)


def _mem_in(x, g, w, mk, mv):
    return pl.pallas_call(
        _mem_in_kernel,
        grid=(SEQ // TM,),
        in_specs=[pl.BlockSpec((TM, D_MODEL), lambda i: (i, 0)), _full((1, D_MODEL)),
                  _full(w.shape), _full(mk.shape), _full(mv.shape)],
        out_specs=pl.BlockSpec((TM, MEM_WIDTH), lambda i: (i, 0)),
        out_shape=jax.ShapeDtypeStruct((SEQ, MEM_WIDTH), BF),
        compiler_params=_cp(1),
        name="mem_in",
    )(x, g, w, mk, mv)


def _proj_kernel(x_ref, g_ref, w_ref, o_ref):
    h = _rms(x_ref[...], g_ref[...]).astype(BF)
    o_ref[...] = jnp.dot(h, w_ref[...], preferred_element_type=F32).astype(BF)


def _dil_proj(x, g, w, d):
    sub = SEQ // d
    nb = sub // TM
    n = w.shape[1]
    return pl.pallas_call(
        _proj_kernel,
        grid=(d, nb),
        in_specs=[pl.BlockSpec((TM, D_MODEL), lambda r, j: (j, r)), _full((1, D_MODEL)), _full(w.shape)],
        out_specs=pl.BlockSpec((TM, n), lambda r, j: (r * nb + j, 0)),
        out_shape=jax.ShapeDtypeStruct((SEQ, n), BF),
        compiler_params=_cp(2),
        name=f"dil_proj_d{d}",
    )(x.reshape(sub, d * D_MODEL), g, w)


def _band_kernel(q_ref, k_ref, v_ref, kp_ref, vp_ref, kn_ref, vn_ref, o_ref, lse_ref, kbuf, vbuf,
                 *, dil, sub_len, slopes):
    j = pl.program_id(1)
    hb = BAND_HALF
    kbuf[0:hb] = kp_ref[...]
    kbuf[hb:hb + BAND_TB] = k_ref[...]
    kbuf[hb + BAND_TB:] = kn_ref[...]
    vbuf[0:hb] = vp_ref[...]
    vbuf[hb:hb + BAND_TB] = v_ref[...]
    vbuf[hb + BAND_TB:] = vn_ref[...]
    win = BAND_SB + 2 * hb
    qi = lax.broadcasted_iota(jnp.int32, (BAND_SB, win), 0)
    kj = lax.broadcasted_iota(jnp.int32, (BAND_SB, win), 1)
    rel = jnp.abs(kj - hb - qi)
    dist = rel.astype(F32) * float(dil)
    lane = lax.broadcasted_iota(jnp.int32, (BAND_SB, LANES), 1)
    for sb in range(BAND_TB // BAND_SB):
        r0 = sb * BAND_SB
        key_idx = j * BAND_TB + (r0 - hb) + kj
        valid = (rel <= hb) & (key_idx >= 0) & (key_idx < sub_len)
        for hp in range(DIL_W // LANES):
            cs = slice(hp * LANES, (hp + 1) * LANES)
            q2 = q_ref[r0:r0 + BAND_SB, cs].astype(F32)
            k2 = kbuf[r0:r0 + win, cs]
            v2 = vbuf[r0:r0 + win, cs]
            o_pair = jnp.zeros((BAND_SB, LANES), F32)
            l_pair = jnp.zeros((BAND_SB, LANES), F32)
            for hh in range(2):
                lm = (lane >= hh * HEAD_DIM) & (lane < (hh + 1) * HEAD_DIM)
                qh = jnp.where(lm, q2, 0.0).astype(BF)
                s = lax.dot_general(qh, k2, NT_DIMS, preferred_element_type=F32) * (HEAD_DIM ** -0.5)
                s = s - slopes[2 * hp + hh] * dist
                s = jnp.where(valid, s, NEG)
                m = jnp.max(s, axis=-1, keepdims=True)
                p = jnp.exp(s - m)
                l = jnp.sum(p, axis=-1, keepdims=True)
                o = jnp.dot(p.astype(BF), v2, preferred_element_type=F32) * (1.0 / l)
                o_pair = jnp.where(lm, o, o_pair)
                l_pair = jnp.where(lm, m + jnp.log(l), l_pair)
            o_ref[r0:r0 + BAND_SB, cs] = o_pair.astype(BF)
            lse_ref[r0:r0 + BAND_SB, cs] = l_pair


def _band_attn(qkv, g_idx, d):
    sub = SEQ // d
    nb = sub // BAND_TB
    hpb = BAND_TB // BAND_HALF
    n_halo = SEQ // BAND_HALF
    n_heads = len(DIL_GROUPS) * DIL_HEADS
    slopes = tuple(2.0 ** (-ALIBI_MAX * (g_idx * DIL_HEADS + h + 1.0) / n_heads) for h in range(DIL_HEADS))

    def main(col):
        return pl.BlockSpec((BAND_TB, DIL_W), lambda r, j: (r * nb + j, col))

    def prev(col):
        return pl.BlockSpec((BAND_HALF, DIL_W), lambda r, j: (jnp.maximum((r * nb + j) * hpb - 1, 0), col))

    def nxt(col):
        return pl.BlockSpec((BAND_HALF, DIL_W),
                            lambda r, j: (jnp.minimum((r * nb + j + 1) * hpb, n_halo - 1), col))

    ospec = pl.BlockSpec((BAND_TB, DIL_W), lambda r, j: (j, r))
    o, lse = pl.pallas_call(
        functools.partial(_band_kernel, dil=d, sub_len=sub, slopes=slopes),
        grid=(d, nb),
        in_specs=[main(0), main(1), main(2), prev(1), prev(2), nxt(1), nxt(2)],
        out_specs=[ospec, ospec],
        out_shape=[jax.ShapeDtypeStruct((sub, d * DIL_W), BF), jax.ShapeDtypeStruct((sub, d * DIL_W), F32)],
        scratch_shapes=[pltpu.VMEM((BAND_TB + 2 * BAND_HALF, DIL_W), BF),
                        pltpu.VMEM((BAND_TB + 2 * BAND_HALF, DIL_W), BF)],
        compiler_params=_cp(2),
        name=f"band_attn_d{d}",
    )(qkv, qkv, qkv, qkv, qkv, qkv, qkv)
    return o.reshape(SEQ, DIL_W), lse.reshape(SEQ, DIL_W)


def _dil_out_kernel(o0, o1, o2, l0, l1, l2, mo_ref, x_ref, w_ref, g_ref, out_ref):
    a0, a1, a2 = l0[...], l1[...], l2[...]
    mx = jnp.maximum(jnp.maximum(a0, a1), a2)
    e0, e1, e2 = jnp.exp(a0 - mx), jnp.exp(a1 - mx), jnp.exp(a2 - mx)
    inv = 1.0 / (e0 + e1 + e2)
    o = (e0 * inv) * o0[...].astype(F32) + (e1 * inv) * o1[...].astype(F32) + (e2 * inv) * o2[...].astype(F32)
    y = jnp.dot(o.astype(BF), w_ref[:DIL_W, :], preferred_element_type=F32)
    y = y + jnp.dot(mo_ref[...], w_ref[DIL_W:, :], preferred_element_type=F32)
    out_ref[...] = x_ref[...] + _rms(y, g_ref[...])


def _dil_out(os_, ls_, mo, x, w, g):
    row = pl.BlockSpec((TM, D_MODEL), lambda i: (i, 0))
    half = pl.BlockSpec((TM, DIL_W), lambda i: (i, 0))
    return pl.pallas_call(
        _dil_out_kernel,
        grid=(SEQ // TM,),
        in_specs=[half] * 6 + [pl.BlockSpec((TM, MEM_WIDTH), lambda i: (i, 0)), row,
                               _full(w.shape), _full((1, D_MODEL))],
        out_specs=row,
        out_shape=jax.ShapeDtypeStruct((SEQ, D_MODEL), F32),
        compiler_params=_cp(1),
        name="dil_out",
    )(*os_, *ls_, mo, x, w, g)


def _conf_in_kernel(x_ref, g_ref, w_ref, mk_ref, mv_ref, u_ref, mo_ref):
    h = _rms(x_ref[...], g_ref[...]).astype(BF)
    z = jnp.dot(h, w_ref[...], preferred_element_type=F32)
    a, gate = z[:, :CONV_CH], z[:, CONV_CH:2 * CONV_CH]
    u_ref[...] = a / (1.0 + jnp.exp(-gate))
    mo_ref[...] = _mem_attn(z[:, 2 * CONV_CH:], mk_ref, mv_ref).astype(BF)


def _conf_in(x, g, w, mk, mv):
    return pl.pallas_call(
        _conf_in_kernel,
        grid=(SEQ // TM,),
        in_specs=[pl.BlockSpec((TM, D_MODEL), lambda i: (i, 0)), _full((1, D_MODEL)),
                  _full(w.shape), _full(mk.shape), _full(mv.shape)],
        out_specs=[pl.BlockSpec((TM, CONV_CH), lambda i: (i, 0)),
                   pl.BlockSpec((TM, MEM_WIDTH), lambda i: (i, 0))],
        out_shape=[jax.ShapeDtypeStruct((SEQ, CONV_CH), F32), jax.ShapeDtypeStruct((SEQ, MEM_WIDTH), BF)],
        compiler_params=_cp(1),
        name="conf_in",
    )(x, g, w, mk, mv)


def _sc_in_kernel(x_ref, g_ref, w_ref, mk_ref, mv_ref, bg_ref, v_ref, mo_ref):
    h = _rms(x_ref[...], g_ref[...]).astype(BF)
    z = jnp.dot(h, w_ref[...], preferred_element_type=F32)
    bg_ref[...] = z[:, :SC_CH]
    v_ref[...] = z[:, SC_CH:2 * SC_CH] * z[:, 2 * SC_CH:3 * SC_CH]
    mo_ref[...] = _mem_attn(z[:, 3 * SC_CH:], mk_ref, mv_ref).astype(BF)


def _sc_in(x, g, w, mk, mv):
    ch = pl.BlockSpec((TM, SC_CH), lambda i: (i, 0))
    chs = jax.ShapeDtypeStruct((SEQ, SC_CH), F32)
    return pl.pallas_call(
        _sc_in_kernel,
        grid=(SEQ // TM,),
        in_specs=[pl.BlockSpec((TM, D_MODEL), lambda i: (i, 0)), _full((1, D_MODEL)),
                  _full(w.shape), _full(mk.shape), _full(mv.shape)],
        out_specs=[ch, ch, pl.BlockSpec((TM, MEM_WIDTH), lambda i: (i, 0))],
        out_shape=[chs, chs, jax.ShapeDtypeStruct((SEQ, MEM_WIDTH), BF)],
        compiler_params=_cp(1),
        name="sc_in",
    )(x, g, w, mk, mv)


def _fill_window(win, u_ref, up_ref, un_ref, halo):
    i = pl.program_id(0)
    win[0:halo] = jnp.where(i > 0, up_ref[...], 0.0)
    win[halo:halo + TM] = u_ref[...]
    win[halo + TM:] = jnp.where(i < pl.num_programs(0) - 1, un_ref[...], 0.0)


def _dwconv_chunk(win, cw_ref, r0, width, halo):
    base = r0 + halo - width // 2
    acc = cw_ref[0:1, :] * win[base:base + CONV_ROWS, :]
    for k in range(1, width):
        acc = acc + cw_ref[k:k + 1, :] * win[base + k:base + k + CONV_ROWS, :]
    return acc


def _conf_out_kernel(u_ref, up_ref, un_ref, cw_ref, cb_ref, lg_ref, lb_ref, mo_ref, x_ref, w_ref, g_ref,
                     out_ref, win, o_sc, *, halo):
    _fill_window(win, u_ref, up_ref, un_ref, halo)
    for c in range(TM // CONV_ROWS):
        r0 = c * CONV_ROWS
        t = _dwconv_chunk(win, cw_ref, r0, CONV_WIDTH, halo) + cb_ref[...]
        mu = jnp.mean(t, axis=-1, keepdims=True)
        tc = t - mu
        var = jnp.mean(tc * tc, axis=-1, keepdims=True)
        t = tc * lax.rsqrt(var + EPS) * lg_ref[...] + lb_ref[...]
        o_sc[r0:r0 + CONV_ROWS, :] = (t / (1.0 + jnp.exp(-t))).astype(BF)
    y = jnp.dot(o_sc[...], w_ref[:CONV_CH, :], preferred_element_type=F32)
    y = y + jnp.dot(mo_ref[...], w_ref[CONV_CH:, :], preferred_element_type=F32)
    out_ref[...] = x_ref[...] + _rms(y, g_ref[...])


def _sc_out_kernel(v_ref, vp_ref, vn_ref, cw_ref, bg_ref, mo_ref, x_ref, w_ref, g_ref,
                   out_ref, win, o_sc, *, halo):
    _fill_window(win, v_ref, vp_ref, vn_ref, halo)
    for c in range(TM // CONV_ROWS):
        r0 = c * CONV_ROWS
        t = _dwconv_chunk(win, cw_ref, r0, SC_WIDTH, halo)
        o_sc[r0:r0 + CONV_ROWS, :] = (bg_ref[r0:r0 + CONV_ROWS, :] * t).astype(BF)
    y = jnp.dot(o_sc[...], w_ref[:SC_CH, :], preferred_element_type=F32)
    y = y + jnp.dot(mo_ref[...], w_ref[SC_CH:, :], preferred_element_type=F32)
    out_ref[...] = x_ref[...] + _rms(y, g_ref[...])


def _conv_specs(ch, halo):
    hpb = TM // halo
    n_halo = SEQ // halo
    return [pl.BlockSpec((TM, ch), lambda i: (i, 0)),
            pl.BlockSpec((halo, ch), lambda i: (jnp.maximum(i * hpb - 1, 0), 0)),
            pl.BlockSpec((halo, ch), lambda i: (jnp.minimum((i + 1) * hpb, n_halo - 1), 0))]


def _conf_out(u, cw, cb, lg, lb, mo, x, w, g):
    halo = 16
    row = pl.BlockSpec((TM, D_MODEL), lambda i: (i, 0))
    return pl.pallas_call(
        functools.partial(_conf_out_kernel, halo=halo),
        grid=(SEQ // TM,),
        in_specs=_conv_specs(CONV_CH, halo) + [
            _full(cw.shape), _full(cb.shape), _full(lg.shape), _full(lb.shape),
            pl.BlockSpec((TM, MEM_WIDTH), lambda i: (i, 0)), row, _full(w.shape), _full((1, D_MODEL))],
        out_specs=row,
        out_shape=jax.ShapeDtypeStruct((SEQ, D_MODEL), F32),
        scratch_shapes=[pltpu.VMEM((TM + 2 * halo, CONV_CH), F32), pltpu.VMEM((TM, CONV_CH), BF)],
        compiler_params=_cp(1),
        name="conf_out",
    )(u, u, u, cw, cb, lg, lb, mo, x, w, g)


def _sc_out(v, cw, bg, mo, x, w, g):
    halo = 8
    row = pl.BlockSpec((TM, D_MODEL), lambda i: (i, 0))
    return pl.pallas_call(
        functools.partial(_sc_out_kernel, halo=halo),
        grid=(SEQ // TM,),
        in_specs=_conv_specs(SC_CH, halo) + [
            _full(cw.shape), pl.BlockSpec((TM, SC_CH), lambda i: (i, 0)),
            pl.BlockSpec((TM, MEM_WIDTH), lambda i: (i, 0)), row, _full(w.shape), _full((1, D_MODEL))],
        out_specs=row,
        out_shape=jax.ShapeDtypeStruct((SEQ, D_MODEL), F32),
        scratch_shapes=[pltpu.VMEM((TM + 2 * halo, SC_CH), F32), pltpu.VMEM((TM, SC_CH), BF)],
        compiler_params=_cp(1),
        name="sc_out",
    )(v, v, v, cw, bg, mo, x, w, g)


def _mla_weights(w_in, w_uq, w_ukv):
    o1 = MLA_Q_LORA
    o2 = o1 + MLA_KV_LORA
    o3 = o2 + MLA_ROPE
    zeros = lambda n: jnp.zeros((D_MODEL, n), w_in.dtype)
    win = jnp.concatenate([w_in[:, :o2], zeros(MLA_V), w_in[:, o2:o3], zeros(LANES - MLA_V - MLA_ROPE),
                           w_in[:, o3:]], axis=1).astype(BF)
    qd = MLA_NOPE + MLA_ROPE
    wuq = jnp.pad(w_uq.reshape(MLA_Q_LORA, MLA_HEADS, qd), ((0, 0), (0, 0), (0, LANES - qd)))
    wuq = wuq.reshape(MLA_Q_LORA, MLA_HEADS * LANES).astype(BF)
    wkv = w_ukv.reshape(MLA_KV_LORA, MLA_HEADS, MLA_NOPE + MLA_V)
    pad = ((0, 0), (0, 0), (0, LANES - MLA_NOPE))
    wuk = jnp.pad(wkv[..., :MLA_NOPE], pad).reshape(MLA_KV_LORA, MLA_HEADS * LANES).astype(BF)
    wuv = jnp.pad(wkv[..., MLA_NOPE:], pad).reshape(MLA_KV_LORA, MLA_HEADS * LANES)
    return win, wuq, wuk, wuv.T.astype(BF)


def _rope_tables():
    half = MLA_ROPE // 2
    inv = ROPE_THETA ** (-jnp.arange(half, dtype=F32) / half)
    z = lambda n: jnp.zeros((n,), F32)
    invf = jnp.concatenate([z(MLA_NOPE), inv, inv, z(LANES - MLA_NOPE - MLA_ROPE)])
    sgn = jnp.concatenate([z(MLA_NOPE), -jnp.ones((half,), F32), jnp.ones((half,), F32),
                           z(LANES - MLA_NOPE - MLA_ROPE)])
    return invf[None, :], sgn[None, :]


def kernel(x, mem, positions, norm_g, ffn_w_gate, ffn_w_up, ffn_w_down, mem_w_kv, a_w_in, a_q_norm, a_kv_norm, a_w_uq, a_w_ukv, a_w_out, b_w_in, b_w_out, c_w_in, c_conv_w, c_conv_b, c_ln_g, c_ln_b, c_w_out, d_w_in, d_conv_w, d_w_out):
    assert x.shape == (1, SEQ, D_MODEL) and mem.shape == (1, N_MEM, D_MODEL)
    xs = x[0]
    pos = positions.reshape(SEQ, 1)
    gn = lambda i, j: norm_g[i, j][None, :]
    mem_k, mem_v = _memkv(mem[0], norm_g[:, 6][:, None, :], mem_w_kv.astype(BF))
    invf, sgn = _rope_tables()
    n_mix = 4
    for i in range(DEPTH):
        xs = _ffn(xs, gn(i, 0), gn(i, 1), ffn_w_gate[i, 0].astype(BF), ffn_w_up[i, 0].astype(BF),
                  ffn_w_down[i, 0].astype(BF))
        m, j = i % n_mix, i // n_mix
        mk, mv = mem_k[i], mem_v[i]
        if m == 0:
            win, wuq, wuk, wuvt = _mla_weights(a_w_in[j], a_w_uq[j], a_w_ukv[j])
            q, k, vt, mo = _mla_in(xs, pos, gn(i, 2), win, a_q_norm[j][None, :], a_kv_norm[j][None, :],
                                   wuq, wuk, wuvt, invf, sgn, mk, mv)
            o = _flash(q, k, vt)
            xs = _out_proj(o, mo, xs, a_w_out[j].astype(BF), gn(i, 3))
        elif m == 1:
            w = b_w_in[j].astype(BF)
            gw = 3 * DIL_W
            mo = _mem_in(xs, gn(i, 2), w[:, len(DIL_GROUPS) * gw:], mk, mv)
            os_, ls_ = [], []
            for gi, (_, d) in enumerate(DIL_GROUPS):
                qkv = _dil_proj(xs, gn(i, 2), w[:, gi * gw:(gi + 1) * gw], d)
                o, lse = _band_attn(qkv, gi, d)
                os_.append(o)
                ls_.append(lse)
            xs = _dil_out(os_, ls_, mo, xs, b_w_out[j].astype(BF), gn(i, 3))
        elif m == 2:
            u, mo = _conf_in(xs, gn(i, 2), c_w_in[j].astype(BF), mk, mv)
            xs = _conf_out(u, c_conv_w[j], c_conv_b[j][None, :], c_ln_g[j][None, :], c_ln_b[j][None, :],
                           mo, xs, c_w_out[j].astype(BF), gn(i, 3))
        else:
            bg, v, mo = _sc_in(xs, gn(i, 2), d_w_in[j].astype(BF), mk, mv)
            xs = _sc_out(v, d_conv_w[j], bg, mo, xs, d_w_out[j].astype(BF), gn(i, 3))
        xs = _ffn(xs, gn(i, 4), gn(i, 5), ffn_w_gate[i, 1].astype(BF), ffn_w_up[i, 1].astype(BF),
                  ffn_w_down[i, 1].astype(BF))
    return xs[None]
```

```python
import functools
import math

import jax
import jax.numpy as jnp
from jax import lax
from jax.experimental import pallas as pl
from jax.experimental.pallas import tpu as pltpu

F32 = jnp.float32
BF = jnp.bfloat16

D_MODEL = 1024
SEQ = 16384
DEPTH = 4
N_MEM = 256
HEAD_DIM = 64
MEM_HEADS = 4
MEM_WIDTH = MEM_HEADS * HEAD_DIM
MLA_HEADS = 12
MLA_Q_LORA = 384
MLA_KV_LORA = 256
MLA_NOPE = 64
MLA_ROPE = 32
MLA_V = 64
ROPE_THETA = 10000.0
DIL_GROUPS = ((128, 1), (512, 4), (2048, 16))
DIL_HEADS = 8
DIL_W = DIL_HEADS * HEAD_DIM
ALIBI_MAX = 8.0
CONV_CH = 768
CONV_WIDTH = 31
SC_CH = 768
SC_WIDTH = 3
D_FF = 2816
EPS = 1e-6
NEG = -1e30

LANES = 128
SUBLANES = 8
CONV_CHUNK = 128
TM = 512
TQ = 512
TK = 512
BAND_TB = 512
BAND_SB = 128
BAND_HALF = 64
CONV_ROWS = 32
VMEM_LIMIT = 56 << 20

NT_DIMS = (((1,), (1,)), ((), ()))


def _cp(n_axes):
    return pltpu.CompilerParams(dimension_semantics=("arbitrary",) * n_axes,
                                vmem_limit_bytes=VMEM_LIMIT)


def _full(shape):
    return pl.BlockSpec(shape, lambda *_: (0,) * len(shape), pipeline_mode=pl.Buffered(1))


def _rms(x, g):
    return x * lax.rsqrt(jnp.mean(x * x, axis=-1, keepdims=True) + EPS) * g


def _mem_attn(qm, mk_ref, mv_ref):
    kk = mk_ref[...]
    vv = mv_ref[...]
    lane = lax.broadcasted_iota(jnp.int32, qm.shape, 1)
    out = jnp.zeros_like(qm)
    for h in range(MEM_HEADS):
        msk = (lane >= h * HEAD_DIM) & (lane < (h + 1) * HEAD_DIM)
        qh = jnp.where(msk, qm, 0.0).astype(BF)
        s = lax.dot_general(qh, kk, NT_DIMS, preferred_element_type=F32) * (HEAD_DIM ** -0.5)
        m = jnp.max(s, axis=-1, keepdims=True)
        p = jnp.exp(s - m)
        l = jnp.sum(p, axis=-1, keepdims=True)
        o = jnp.dot(p.astype(BF), vv, preferred_element_type=F32) * (1.0 / l)
        out = jnp.where(msk, o, out)
    return out


def _ffn_kernel(x_ref, gi_ref, go_ref, wg_ref, wu_ref, wd_ref, o_ref):
    x = x_ref[...]
    h = _rms(x, gi_ref[...]).astype(BF)
    g = jnp.dot(h, wg_ref[...], preferred_element_type=F32)
    u = jnp.dot(h, wu_ref[...], preferred_element_type=F32)
    a = (g / (1.0 + jnp.exp(-g)) * u).astype(BF)
    y = jnp.dot(a, wd_ref[...], preferred_element_type=F32)
    o_ref[...] = x + 0.5 * _rms(y, go_ref[...])


def _ffn(x, gi, go, wg, wu, wd):
    row = pl.BlockSpec((TM, D_MODEL), lambda i: (i, 0))
    return pl.pallas_call(
        _ffn_kernel,
        grid=(SEQ // TM,),
        in_specs=[row, _full((1, D_MODEL)), _full((1, D_MODEL)),
                  _full((D_MODEL, D_FF)), _full((D_MODEL, D_FF)), _full((D_FF, D_MODEL))],
        out_specs=row,
        out_shape=jax.ShapeDtypeStruct((SEQ, D_MODEL), F32),
        compiler_params=_cp(1),
        name="ffn",
    )(x, gi, go, wg, wu, wd)


def _memkv_kernel(mem_ref, g_ref, w_ref, k_ref, v_ref):
    mn = _rms(mem_ref[...], g_ref[0]).astype(BF)
    kv = jnp.dot(mn, w_ref[0], preferred_element_type=F32)
    k_ref[0] = kv[:, :MEM_WIDTH].astype(BF)
    v_ref[0] = kv[:, MEM_WIDTH:].astype(BF)


def _memkv(mem, g6, w_kv):
    out = jax.ShapeDtypeStruct((DEPTH, N_MEM, MEM_WIDTH), BF)
    ospec = pl.BlockSpec((1, N_MEM, MEM_WIDTH), lambda i: (i, 0, 0))
    return pl.pallas_call(
        _memkv_kernel,
        grid=(DEPTH,),
        in_specs=[_full((N_MEM, D_MODEL)),
                  pl.BlockSpec((1, 1, D_MODEL), lambda i: (i, 0, 0)),
                  pl.BlockSpec((1, D_MODEL, 2 * MEM_WIDTH), lambda i: (i, 0, 0))],
        out_specs=[ospec, ospec],
        out_shape=[out, out],
        compiler_params=_cp(1),
        name="memkv",
    )(mem, g6, w_kv)


def _out_kernel(o_ref, mo_ref, x_ref, w_ref, g_ref, out_ref, *, wo):
    y = jnp.dot(o_ref[...], w_ref[:wo, :], preferred_element_type=F32)
    y = y + jnp.dot(mo_ref[...], w_ref[wo:, :], preferred_element_type=F32)
    out_ref[...] = x_ref[...] + _rms(y, g_ref[...])


def _out_proj(o, mo, x, w, g):
    wo = o.shape[1]
    row = pl.BlockSpec((TM, D_MODEL), lambda i: (i, 0))
    return pl.pallas_call(
        functools.partial(_out_kernel, wo=wo),
        grid=(SEQ // TM,),
        in_specs=[pl.BlockSpec((TM, wo), lambda i: (i, 0)),
                  pl.BlockSpec((TM, MEM_WIDTH), lambda i: (i, 0)),
                  row, _full(w.shape), _full((1, D_MODEL))],
        out_specs=row,
        out_shape=jax.ShapeDtypeStruct((SEQ, D_MODEL), F32),
        compiler_params=_cp(1),
        name="out_proj",
    )(o, mo, x, w, g)


def _rope(t, cos, sin_signed, lane):
    partner = jnp.where(lane < 80, pltpu.roll(t, LANES - 16, 1), pltpu.roll(t, 16, 1))
    return t * cos + partner * sin_signed


def _mla_in_kernel(x_ref, pos_ref, g_ref, win_ref, qg_ref, kvg_ref, wuq_ref, wuk_ref, wuvt_ref,
                   invf_ref, sgn_ref, mk_ref, mv_ref, q_ref, k_ref, vt_ref, mo_ref):
    h = _rms(x_ref[...], g_ref[...]).astype(BF)
    z = jnp.dot(h, win_ref[...], preferred_element_type=F32)
    o1, o2, o3 = MLA_Q_LORA, MLA_Q_LORA + MLA_KV_LORA, MLA_Q_LORA + MLA_KV_LORA + LANES
    cqn = _rms(z[:, :o1], qg_ref[...]).astype(BF)
    ckvn = _rms(z[:, o1:o2], kvg_ref[...]).astype(BF)
    krg = z[:, o2:o3]
    mo_ref[...] = _mem_attn(z[:, o3:], mk_ref, mv_ref).astype(BF)

    ang = pos_ref[...].astype(F32) * invf_ref[...]
    cos = jnp.cos(ang)
    sin_signed = jnp.sin(ang) * sgn_ref[...]
    lane = lax.broadcasted_iota(jnp.int32, (TM, LANES), 1)

    q = jnp.dot(cqn, wuq_ref[...], preferred_element_type=F32)
    kn = jnp.dot(ckvn, wuk_ref[...], preferred_element_type=F32)
    vt = lax.dot_general(wuvt_ref[...], ckvn, NT_DIMS, preferred_element_type=F32)
    kr = _rope(krg, cos, sin_signed, lane)
    ones_row = (lax.broadcasted_iota(jnp.int32, (LANES, TM), 0) == MLA_V).astype(F32)
    qscale = (MLA_NOPE + MLA_ROPE) ** -0.5 * math.log2(math.e)
    for hd in range(MLA_HEADS):
        sl = slice(hd * LANES, (hd + 1) * LANES)
        q_ref[hd] = (_rope(q[:, sl], cos, sin_signed, lane) * qscale).astype(BF)
        k_ref[hd] = (kn[:, sl] + kr).astype(BF)
        vt_ref[hd] = (vt[sl, :] + ones_row).astype(BF)


def _mla_in(x, pos, g, win, qg, kvg, wuq, wuk, wuvt, invf, sgn, mk, mv):
    hm = jax.ShapeDtypeStruct((MLA_HEADS, SEQ, LANES), BF)
    return pl.pallas_call(
        _mla_in_kernel,
        grid=(SEQ // TM,),
        in_specs=[pl.BlockSpec((TM, D_MODEL), lambda i: (i, 0)),
                  pl.BlockSpec((TM, 1), lambda i: (i, 0)),
                  _full((1, D_MODEL)), _full(win.shape), _full(qg.shape), _full(kvg.shape),
                  _full(wuq.shape), _full(wuk.shape), _full(wuvt.shape),
                  _full((1, LANES)), _full((1, LANES)), _full(mk.shape), _full(mv.shape)],
        out_specs=[pl.BlockSpec((MLA_HEADS, TM, LANES), lambda i: (0, i, 0)),
                   pl.BlockSpec((MLA_HEADS, TM, LANES), lambda i: (0, i, 0)),
                   pl.BlockSpec((MLA_HEADS, LANES, TM), lambda i: (0, 0, i)),
                   pl.BlockSpec((TM, MEM_WIDTH), lambda i: (i, 0))],
        out_shape=[hm, hm, jax.ShapeDtypeStruct((MLA_HEADS, LANES, SEQ), BF),
                   jax.ShapeDtypeStruct((SEQ, MEM_WIDTH), BF)],
        compiler_params=_cp(1),
        name="mla_in",
    )(x, pos, g, win, qg, kvg, wuq, wuk, wuvt, invf, sgn, mk, mv)


def _flash_kernel(q_ref, k_ref, vt_ref, o_ref, s00, s01, s10, s11, c00, c01, c10, c11, m0, m1, a0, a1):
    s_sc = ((s00, s01), (s10, s11))
    cm_sc = ((c00, c01), (c10, c11))
    m_sc = (m0, m1)
    acc_sc = (a0, a1)
    n_chunks = SEQ // TK

    def scores(c, slot):
        c0 = pl.multiple_of(c * TK, TK)
        for hh in range(2):
            st = lax.dot_general(k_ref[hh, pl.ds(c0, TK), :], q_ref[hh], NT_DIMS,
                                 preferred_element_type=F32)
            s_sc[hh][slot][...] = st
            cm_sc[hh][slot][...] = jnp.max(st, axis=0, keepdims=True)

    def accumulate(c, slot):
        c0 = pl.multiple_of(c * TK, TK)
        for hh in range(2):
            m_prev = m_sc[hh][...]
            m_new = jnp.maximum(m_prev, cm_sc[hh][slot][...])
            alpha = jnp.exp2(m_prev - m_new)
            pt = jnp.exp2(s_sc[hh][slot][...] - m_new).astype(BF)
            acc_sc[hh][...] = alpha * acc_sc[hh][...] + jnp.dot(vt_ref[hh, :, pl.ds(c0, TK)], pt,
                                                                preferred_element_type=F32)
            m_sc[hh][...] = m_new

    for hh in range(2):
        m_sc[hh][...] = jnp.full(m_sc[hh].shape, NEG, F32)
        acc_sc[hh][...] = jnp.zeros(acc_sc[hh].shape, F32)
    scores(0, 0)

    def body(c2, carry):
        c = 2 * c2
        scores(c + 1, 1)
        accumulate(c, 0)
        scores(jnp.minimum(c + 2, n_chunks - 1), 0)
        accumulate(c + 1, 1)
        return carry

    lax.fori_loop(0, n_chunks // 2, body, 0)
    outs = []
    for hh in range(2):
        acc = acc_sc[hh][...]
        outs.append(acc[:MLA_V] * (1.0 / acc[MLA_V:MLA_V + 1]))
    o_ref[...] = jnp.concatenate(outs, axis=0).T.astype(BF)


def _flash(q, k, vt):
    return pl.pallas_call(
        _flash_kernel,
        grid=(MLA_HEADS // 2, SEQ // TQ),
        in_specs=[pl.BlockSpec((2, TQ, LANES), lambda p, i: (p, i, 0)),
                  pl.BlockSpec((2, SEQ, LANES), lambda p, i: (p, 0, 0), pipeline_mode=pl.Buffered(1)),
                  pl.BlockSpec((2, LANES, SEQ), lambda p, i: (p, 0, 0), pipeline_mode=pl.Buffered(1))],
        out_specs=pl.BlockSpec((TQ, LANES), lambda p, i: (i, p)),
        out_shape=jax.ShapeDtypeStruct((SEQ, MLA_HEADS * MLA_V), BF),
        scratch_shapes=([pltpu.VMEM((TK, TQ), F32)] * 4 + [pltpu.VMEM((1, TQ), F32)] * 6
                        + [pltpu.VMEM((LANES, TQ), F32)] * 2),
        compiler_params=_cp(2),
        name="mla_flash",
    )(q, k, vt)


DIL_GW = 3 * DIL_W


def _dil_in_kernel(x_ref, g_ref, w_ref, mk_ref, mv_ref, o0_ref, o1_ref, o2_ref, mo_ref, z1_sc, z2_sc):
    h = _rms(x_ref[...], g_ref[...]).astype(BF)
    outs = (o0_ref, o1_ref, o2_ref)
    stage = (None, z1_sc, z2_sc)
    for gi, (_, d) in enumerate(DIL_GROUPS):
        z = jnp.dot(h, w_ref[:, gi * DIL_GW:(gi + 1) * DIL_GW], preferred_element_type=F32)
        if d == 1:
            outs[gi][0] = z.astype(BF)
        else:
            for c in range(DIL_GW // LANES):
                stage[gi][c] = z[:, c * LANES:(c + 1) * LANES]
            for r in range(d):
                for c in range(DIL_GW // LANES):
                    outs[gi][r, :, c * LANES:(c + 1) * LANES] = (
                        stage[gi][c, pl.ds(r, TM // d, stride=d), :].astype(BF))
    qm = jnp.dot(h, w_ref[:, len(DIL_GROUPS) * DIL_GW:], preferred_element_type=F32)
    mo_ref[...] = _mem_attn(qm, mk_ref, mv_ref).astype(BF)


def _dil_in(x, g, w, mk, mv):
    out_specs, out_shape = [], []
    for _, d in DIL_GROUPS:
        out_specs.append(pl.BlockSpec((d, TM // d, DIL_GW), lambda i: (0, i, 0)))
        out_shape.append(jax.ShapeDtypeStruct((d, SEQ // d, DIL_GW), BF))
    out_specs.append(pl.BlockSpec((TM, MEM_WIDTH), lambda i: (i, 0)))
    out_shape.append(jax.ShapeDtypeStruct((SEQ, MEM_WIDTH), BF))
    return pl.pallas_call(
        _dil_in_kernel,
        grid=(SEQ // TM,),
        in_specs=[pl.BlockSpec((TM, D_MODEL), lambda i: (i, 0)), _full((1, D_MODEL)),
                  _full(w.shape), _full(mk.shape), _full(mv.shape)],
        out_specs=out_specs,
        out_shape=out_shape,
        scratch_shapes=[pltpu.VMEM((DIL_GW // LANES, TM, LANES), F32)] * 2,
        compiler_params=_cp(1),
        name="dil_in",
    )(x, g, w, mk, mv)


def _band_kernel(q_ref, k_ref, v_ref, kp_ref, vp_ref, kn_ref, vn_ref, o_ref, lse_ref, kbuf, vbuf,
                 *, dil, sub_len, slopes):
    j = pl.program_id(1)
    hb = BAND_HALF
    kbuf[0:hb] = kp_ref[...]
    kbuf[hb:hb + BAND_TB] = k_ref[...]
    kbuf[hb + BAND_TB:] = kn_ref[...]
    vbuf[0:hb] = vp_ref[...]
    vbuf[hb:hb + BAND_TB] = v_ref[...]
    vbuf[hb + BAND_TB:] = vn_ref[...]
    win = BAND_SB + 2 * hb
    qi = lax.broadcasted_iota(jnp.int32, (BAND_SB, win), 0)
    kj = lax.broadcasted_iota(jnp.int32, (BAND_SB, win), 1)
    rel = jnp.abs(kj - hb - qi)
    dist = rel.astype(F32) * float(dil)
    lane = lax.broadcasted_iota(jnp.int32, (BAND_SB, LANES), 1)
    for sb in range(BAND_TB // BAND_SB):
        r0 = sb * BAND_SB
        key_idx = j * BAND_TB + (r0 - hb) + kj
        valid = (rel <= hb) & (key_idx >= 0) & (key_idx < sub_len)
        for hp in range(DIL_W // LANES):
            cs = slice(hp * LANES, (hp + 1) * LANES)
            q2 = q_ref[r0:r0 + BAND_SB, cs].astype(F32)
            k2 = kbuf[r0:r0 + win, cs]
            v2 = vbuf[r0:r0 + win, cs]
            o_pair = jnp.zeros((BAND_SB, LANES), F32)
            l_pair = jnp.zeros((BAND_SB, LANES), F32)
            for hh in range(2):
                lm = (lane >= hh * HEAD_DIM) & (lane < (hh + 1) * HEAD_DIM)
                qh = jnp.where(lm, q2, 0.0).astype(BF)
                s = lax.dot_general(qh, k2, NT_DIMS, preferred_element_type=F32) * (HEAD_DIM ** -0.5)
                s = s - slopes[2 * hp + hh] * dist
                s = jnp.where(valid, s, NEG)
                m = jnp.max(s, axis=-1, keepdims=True)
                p = jnp.exp(s - m)
                l = jnp.sum(p, axis=-1, keepdims=True)
                o = jnp.dot(p.astype(BF), v2, preferred_element_type=F32) * (1.0 / l)
                o_pair = jnp.where(lm, o, o_pair)
                l_pair = jnp.where(lm, m + jnp.log(l), l_pair)
            o_ref[r0:r0 + BAND_SB, cs] = o_pair.astype(BF)
            lse_ref[r0:r0 + BAND_SB, cs] = l_pair


def _band_attn(qkv, g_idx, d):
    sub = SEQ // d
    nb = sub // BAND_TB
    hpb = BAND_TB // BAND_HALF
    n_halo = sub // BAND_HALF
    n_heads = len(DIL_GROUPS) * DIL_HEADS
    slopes = tuple(2.0 ** (-ALIBI_MAX * (g_idx * DIL_HEADS + h + 1.0) / n_heads) for h in range(DIL_HEADS))

    def main(col):
        return pl.BlockSpec((None, BAND_TB, DIL_W), lambda r, j: (r, j, col))

    def prev(col):
        return pl.BlockSpec((None, BAND_HALF, DIL_W), lambda r, j: (r, jnp.maximum(j * hpb - 1, 0), col))

    def nxt(col):
        return pl.BlockSpec((None, BAND_HALF, DIL_W),
                            lambda r, j: (r, jnp.minimum((j + 1) * hpb, n_halo - 1), col))

    ospec = pl.BlockSpec((None, BAND_TB, DIL_W), lambda r, j: (r, j, 0))
    return pl.pallas_call(
        functools.partial(_band_kernel, dil=d, sub_len=sub, slopes=slopes),
        grid=(d, nb),
        in_specs=[main(0), main(1), main(2), prev(1), prev(2), nxt(1), nxt(2)],
        out_specs=[ospec, ospec],
        out_shape=[jax.ShapeDtypeStruct((d, sub, DIL_W), BF), jax.ShapeDtypeStruct((d, sub, DIL_W), F32)],
        scratch_shapes=[pltpu.VMEM((BAND_TB + 2 * BAND_HALF, DIL_W), BF),
                        pltpu.VMEM((BAND_TB + 2 * BAND_HALF, DIL_W), BF)],
        compiler_params=_cp(2),
        name=f"band_attn_d{d}",
    )(qkv, qkv, qkv, qkv, qkv, qkv, qkv)


def _dil_out_kernel(o0, o1, o2, l0, l1, l2, mo_ref, x_ref, w_ref, g_ref, out_ref, so1, so2, sl1, sl2):
    nc = DIL_W // LANES

    def token_order(src_ref, stage):
        d = src_ref.shape[0]
        for r in range(d):
            for c in range(nc):
                stage[c, pl.ds(r, TM // d, stride=d), :] = src_ref[r, :, c * LANES:(c + 1) * LANES].astype(F32)
        return jnp.concatenate([stage[c] for c in range(nc)], axis=1)

    a0, a1, a2 = l0[0], token_order(l1, sl1), token_order(l2, sl2)
    mx = jnp.maximum(jnp.maximum(a0, a1), a2)
    e0, e1, e2 = jnp.exp(a0 - mx), jnp.exp(a1 - mx), jnp.exp(a2 - mx)
    inv = 1.0 / (e0 + e1 + e2)
    o = ((e0 * inv) * o0[0].astype(F32) + (e1 * inv) * token_order(o1, so1)
         + (e2 * inv) * token_order(o2, so2))
    y = jnp.dot(o.astype(BF), w_ref[:DIL_W, :], preferred_element_type=F32)
    y = y + jnp.dot(mo_ref[...], w_ref[DIL_W:, :], preferred_element_type=F32)
    out_ref[...] = x_ref[...] + _rms(y, g_ref[...])


def _dil_out(os_, ls_, mo, x, w, g):
    row = pl.BlockSpec((TM, D_MODEL), lambda i: (i, 0))
    grp = [pl.BlockSpec((d, TM // d, DIL_W), lambda i: (0, i, 0)) for _, d in DIL_GROUPS]
    return pl.pallas_call(
        _dil_out_kernel,
        grid=(SEQ // TM,),
        in_specs=grp + grp + [pl.BlockSpec((TM, MEM_WIDTH), lambda i: (i, 0)), row,
                              _full(w.shape), _full((1, D_MODEL))],
        out_specs=row,
        out_shape=jax.ShapeDtypeStruct((SEQ, D_MODEL), F32),
        scratch_shapes=[pltpu.VMEM((DIL_W // LANES, TM, LANES), F32)] * 4,
        compiler_params=_cp(1),
        name="dil_out",
    )(*os_, *ls_, mo, x, w, g)


def _conf_in_kernel(x_ref, g_ref, w_ref, mk_ref, mv_ref, u_ref, mo_ref):
    h = _rms(x_ref[...], g_ref[...]).astype(BF)
    z = jnp.dot(h, w_ref[...], preferred_element_type=F32)
    a, gate = z[:, :CONV_CH], z[:, CONV_CH:2 * CONV_CH]
    u_ref[...] = a / (1.0 + jnp.exp(-gate))
    mo_ref[...] = _mem_attn(z[:, 2 * CONV_CH:], mk_ref, mv_ref).astype(BF)


def _conf_in(x, g, w, mk, mv):
    return pl.pallas_call(
        _conf_in_kernel,
        grid=(SEQ // TM,),
        in_specs=[pl.BlockSpec((TM, D_MODEL), lambda i: (i, 0)), _full((1, D_MODEL)),
                  _full(w.shape), _full(mk.shape), _full(mv.shape)],
        out_specs=[pl.BlockSpec((TM, CONV_CH), lambda i: (i, 0)),
                   pl.BlockSpec((TM, MEM_WIDTH), lambda i: (i, 0))],
        out_shape=[jax.ShapeDtypeStruct((SEQ, CONV_CH), F32), jax.ShapeDtypeStruct((SEQ, MEM_WIDTH), BF)],
        compiler_params=_cp(1),
        name="conf_in",
    )(x, g, w, mk, mv)


def _sc_in_kernel(x_ref, g_ref, w_ref, mk_ref, mv_ref, bg_ref, v_ref, mo_ref):
    h = _rms(x_ref[...], g_ref[...]).astype(BF)
    z = jnp.dot(h, w_ref[...], preferred_element_type=F32)
    bg_ref[...] = z[:, :SC_CH]
    v_ref[...] = z[:, SC_CH:2 * SC_CH] * z[:, 2 * SC_CH:3 * SC_CH]
    mo_ref[...] = _mem_attn(z[:, 3 * SC_CH:], mk_ref, mv_ref).astype(BF)


def _sc_in(x, g, w, mk, mv):
    ch = pl.BlockSpec((TM, SC_CH), lambda i: (i, 0))
    chs = jax.ShapeDtypeStruct((SEQ, SC_CH), F32)
    return pl.pallas_call(
        _sc_in_kernel,
        grid=(SEQ // TM,),
        in_specs=[pl.BlockSpec((TM, D_MODEL), lambda i: (i, 0)), _full((1, D_MODEL)),
                  _full(w.shape), _full(mk.shape), _full(mv.shape)],
        out_specs=[ch, ch, pl.BlockSpec((TM, MEM_WIDTH), lambda i: (i, 0))],
        out_shape=[chs, chs, jax.ShapeDtypeStruct((SEQ, MEM_WIDTH), BF)],
        compiler_params=_cp(1),
        name="sc_in",
    )(x, g, w, mk, mv)


def _fill_window(win, u_ref, up_ref, un_ref, halo):
    i = pl.program_id(0)
    win[0:halo] = jnp.where(i > 0, up_ref[...], 0.0)
    win[halo:halo + TM] = u_ref[...]
    win[halo + TM:] = jnp.where(i < pl.num_programs(0) - 1, un_ref[...], 0.0)


def _dwconv_chunk(win, cw_ref, r0, width, halo):
    base = r0 + halo - width // 2
    acc = cw_ref[0:1, :] * win[base:base + CONV_ROWS, :]
    for k in range(1, width):
        acc = acc + cw_ref[k:k + 1, :] * win[base + k:base + k + CONV_ROWS, :]
    return acc


def _conf_out_kernel(u_ref, up_ref, un_ref, cw_ref, cb_ref, lg_ref, lb_ref, mo_ref, x_ref, w_ref, g_ref,
                     out_ref, win, sh, t_sc, *, halo):
    _fill_window(win, u_ref, up_ref, un_ref, halo)
    first = halo - CONV_WIDTH // 2
    span = sh.shape[1]
    for c in range(TM // CONV_CHUNK):
        r0 = c * CONV_CHUNK
        for b in range(SUBLANES):
            sh[b] = win[r0 + b:r0 + b + span, :]

        def rows(i, carry, r0=r0):
            rr = pl.multiple_of(i * CONV_ROWS, CONV_ROWS)
            t = cb_ref[...]
            for k in range(CONV_WIDTH):
                a, b = divmod(first + k, SUBLANES)
                t = t + cw_ref[k:k + 1, :] * sh[b, pl.ds(rr + SUBLANES * a, CONV_ROWS), :]
            t_sc[pl.ds(r0 + rr, CONV_ROWS), :] = t
            return carry

        lax.fori_loop(0, CONV_CHUNK // CONV_ROWS, rows, 0)
    t = t_sc[...]
    mu = jnp.mean(t, axis=-1, keepdims=True)
    tc = t - mu
    var = jnp.mean(tc * tc, axis=-1, keepdims=True)
    t = tc * lax.rsqrt(var + EPS) * lg_ref[...] + lb_ref[...]
    o = (t / (1.0 + jnp.exp(-t))).astype(BF)
    y = jnp.dot(o, w_ref[:CONV_CH, :], preferred_element_type=F32)
    y = y + jnp.dot(mo_ref[...], w_ref[CONV_CH:, :], preferred_element_type=F32)
    out_ref[...] = x_ref[...] + _rms(y, g_ref[...])


def _sc_out_kernel(v_ref, vp_ref, vn_ref, cw_ref, bg_ref, mo_ref, x_ref, w_ref, g_ref,
                   out_ref, win, o_sc, *, halo):
    _fill_window(win, v_ref, vp_ref, vn_ref, halo)
    for c in range(TM // CONV_ROWS):
        r0 = c * CONV_ROWS
        t = _dwconv_chunk(win, cw_ref, r0, SC_WIDTH, halo)
        o_sc[r0:r0 + CONV_ROWS, :] = (bg_ref[r0:r0 + CONV_ROWS, :] * t).astype(BF)
    y = jnp.dot(o_sc[...], w_ref[:SC_CH, :], preferred_element_type=F32)
    y = y + jnp.dot(mo_ref[...], w_ref[SC_CH:, :], preferred_element_type=F32)
    out_ref[...] = x_ref[...] + _rms(y, g_ref[...])


def _conv_specs(ch, halo):
    hpb = TM // halo
    n_halo = SEQ // halo
    return [pl.BlockSpec((TM, ch), lambda i: (i, 0)),
            pl.BlockSpec((halo, ch), lambda i: (jnp.maximum(i * hpb - 1, 0), 0)),
            pl.BlockSpec((halo, ch), lambda i: (jnp.minimum((i + 1) * hpb, n_halo - 1), 0))]


def _conf_out(u, cw, cb, lg, lb, mo, x, w, g):
    halo = 16
    span_extra = (halo - CONV_WIDTH // 2 + CONV_WIDTH - 1) // SUBLANES * SUBLANES
    row = pl.BlockSpec((TM, D_MODEL), lambda i: (i, 0))
    return pl.pallas_call(
        functools.partial(_conf_out_kernel, halo=halo),
        grid=(SEQ // TM,),
        in_specs=_conv_specs(CONV_CH, halo) + [
            _full(cw.shape), _full(cb.shape), _full(lg.shape), _full(lb.shape),
            pl.BlockSpec((TM, MEM_WIDTH), lambda i: (i, 0)), row, _full(w.shape), _full((1, D_MODEL))],
        out_specs=row,
        out_shape=jax.ShapeDtypeStruct((SEQ, D_MODEL), F32),
        scratch_shapes=[pltpu.VMEM((TM + 2 * halo, CONV_CH), F32),
                        pltpu.VMEM((SUBLANES, CONV_CHUNK + span_extra, CONV_CH), F32),
                        pltpu.VMEM((TM, CONV_CH), F32)],
        compiler_params=_cp(1),
        name="conf_out",
    )(u, u, u, cw, cb, lg, lb, mo, x, w, g)


def _sc_out(v, cw, bg, mo, x, w, g):
    halo = 8
    row = pl.BlockSpec((TM, D_MODEL), lambda i: (i, 0))
    return pl.pallas_call(
        functools.partial(_sc_out_kernel, halo=halo),
        grid=(SEQ // TM,),
        in_specs=_conv_specs(SC_CH, halo) + [
            _full(cw.shape), pl.BlockSpec((TM, SC_CH), lambda i: (i, 0)),
            pl.BlockSpec((TM, MEM_WIDTH), lambda i: (i, 0)), row, _full(w.shape), _full((1, D_MODEL))],
        out_specs=row,
        out_shape=jax.ShapeDtypeStruct((SEQ, D_MODEL), F32),
        scratch_shapes=[pltpu.VMEM((TM + 2 * halo, SC_CH), F32), pltpu.VMEM((TM, SC_CH), BF)],
        compiler_params=_cp(1),
        name="sc_out",
    )(v, v, v, cw, bg, mo, x, w, g)


def _mla_weights(w_in, w_uq, w_ukv):
    o1 = MLA_Q_LORA
    o2 = o1 + MLA_KV_LORA
    o3 = o2 + MLA_ROPE
    zeros = lambda n: jnp.zeros((D_MODEL, n), w_in.dtype)
    win = jnp.concatenate([w_in[:, :o2], zeros(MLA_V), w_in[:, o2:o3], zeros(LANES - MLA_V - MLA_ROPE),
                           w_in[:, o3:]], axis=1).astype(BF)
    qd = MLA_NOPE + MLA_ROPE
    wuq = jnp.pad(w_uq.reshape(MLA_Q_LORA, MLA_HEADS, qd), ((0, 0), (0, 0), (0, LANES - qd)))
    wuq = wuq.reshape(MLA_Q_LORA, MLA_HEADS * LANES).astype(BF)
    wkv = w_ukv.reshape(MLA_KV_LORA, MLA_HEADS, MLA_NOPE + MLA_V)
    pad = ((0, 0), (0, 0), (0, LANES - MLA_NOPE))
    wuk = jnp.pad(wkv[..., :MLA_NOPE], pad).reshape(MLA_KV_LORA, MLA_HEADS * LANES).astype(BF)
    wuv = jnp.pad(wkv[..., MLA_NOPE:], pad).reshape(MLA_KV_LORA, MLA_HEADS * LANES)
    return win, wuq, wuk, wuv.T.astype(BF)


def _rope_tables():
    half = MLA_ROPE // 2
    inv = ROPE_THETA ** (-jnp.arange(half, dtype=F32) / half)
    z = lambda n: jnp.zeros((n,), F32)
    invf = jnp.concatenate([z(MLA_NOPE), inv, inv, z(LANES - MLA_NOPE - MLA_ROPE)])
    sgn = jnp.concatenate([z(MLA_NOPE), -jnp.ones((half,), F32), jnp.ones((half,), F32),
                           z(LANES - MLA_NOPE - MLA_ROPE)])
    return invf[None, :], sgn[None, :]


def kernel(x, mem, positions, norm_g, ffn_w_gate, ffn_w_up, ffn_w_down, mem_w_kv, a_w_in, a_q_norm, a_kv_norm, a_w_uq, a_w_ukv, a_w_out, b_w_in, b_w_out, c_w_in, c_conv_w, c_conv_b, c_ln_g, c_ln_b, c_w_out, d_w_in, d_conv_w, d_w_out):
    assert x.shape == (1, SEQ, D_MODEL) and mem.shape == (1, N_MEM, D_MODEL)
    xs = x[0]
    pos = positions.reshape(SEQ, 1)
    gn = lambda i, j: norm_g[i, j][None, :]
    mem_k, mem_v = _memkv(mem[0], norm_g[:, 6][:, None, :], mem_w_kv.astype(BF))
    invf, sgn = _rope_tables()
    n_mix = 4
    for i in range(DEPTH):
        xs = _ffn(xs, gn(i, 0), gn(i, 1), ffn_w_gate[i, 0].astype(BF), ffn_w_up[i, 0].astype(BF),
                  ffn_w_down[i, 0].astype(BF))
        m, j = i % n_mix, i // n_mix
        mk, mv = mem_k[i], mem_v[i]
        if m == 0:
            win, wuq, wuk, wuvt = _mla_weights(a_w_in[j], a_w_uq[j], a_w_ukv[j])
            q, k, vt, mo = _mla_in(xs, pos, gn(i, 2), win, a_q_norm[j][None, :], a_kv_norm[j][None, :],
                                   wuq, wuk, wuvt, invf, sgn, mk, mv)
            o = _flash(q, k, vt)
            xs = _out_proj(o, mo, xs, a_w_out[j].astype(BF), gn(i, 3))
        elif m == 1:
            *qkvs, mo = _dil_in(xs, gn(i, 2), b_w_in[j].astype(BF), mk, mv)
            os_, ls_ = [], []
            for gi, (_, d) in enumerate(DIL_GROUPS):
                o, lse = _band_attn(qkvs[gi], gi, d)
                os_.append(o)
                ls_.append(lse)
            xs = _dil_out(os_, ls_, mo, xs, b_w_out[j].astype(BF), gn(i, 3))
        elif m == 2:
            u, mo = _conf_in(xs, gn(i, 2), c_w_in[j].astype(BF), mk, mv)
            xs = _conf_out(u, c_conv_w[j], c_conv_b[j][None, :], c_ln_g[j][None, :], c_ln_b[j][None, :],
                           mo, xs, c_w_out[j].astype(BF), gn(i, 3))
        else:
            bg, v, mo = _sc_in(xs, gn(i, 2), d_w_in[j].astype(BF), mk, mv)
            xs = _sc_out(v, d_conv_w[j], bg, mo, xs, d_w_out[j].astype(BF), gn(i, 3))
        xs = _ffn(xs, gn(i, 4), gn(i, 5), ffn_w_gate[i, 1].astype(BF), ffn_w_up[i, 1].astype(BF),
                  ffn_w_down[i, 1].astype(BF))
    return xs[None]
```

```python
import functools
import math

import jax
import jax.numpy as jnp
from jax import lax
from jax.experimental import pallas as pl
from jax.experimental.pallas import tpu as pltpu

F32 = jnp.float32
BF = jnp.bfloat16

D_MODEL = 1024
SEQ = 16384
DEPTH = 4
N_MEM = 256
HEAD_DIM = 64
MEM_HEADS = 4
MEM_WIDTH = MEM_HEADS * HEAD_DIM
MLA_HEADS = 12
MLA_Q_LORA = 384
MLA_KV_LORA = 256
MLA_NOPE = 64
MLA_ROPE = 32
MLA_V = 64
ROPE_THETA = 10000.0
DIL_GROUPS = ((128, 1), (512, 4), (2048, 16))
DIL_HEADS = 8
DIL_W = DIL_HEADS * HEAD_DIM
ALIBI_MAX = 8.0
CONV_CH = 768
CONV_WIDTH = 31
SC_CH = 768
SC_WIDTH = 3
D_FF = 2816
EPS = 1e-6
NEG = -1e30

LANES = 128
SUBLANES = 8
CONV_CHUNK = 128
TM = 512
TQ = 512
TK = 1024
REF_LANE = MLA_NOPE + MLA_ROPE
FLASH_REBASE = 32.0
BAND_TB = 512
BAND_SB = 128
BAND_HALF = 64
CONV_ROWS = 32
VMEM_LIMIT = 56 << 20

NT_DIMS = (((1,), (1,)), ((), ()))


def _cp(n_axes):
    return pltpu.CompilerParams(dimension_semantics=("arbitrary",) * n_axes,
                                vmem_limit_bytes=VMEM_LIMIT)


def _full(shape):
    return pl.BlockSpec(shape, lambda *_: (0,) * len(shape), pipeline_mode=pl.Buffered(1))


def _rms(x, g):
    return x * lax.rsqrt(jnp.mean(x * x, axis=-1, keepdims=True) + EPS) * g


def _mem_attn(qm, mk_ref, mv_ref):
    kk = mk_ref[...]
    vv = mv_ref[...]
    lane = lax.broadcasted_iota(jnp.int32, qm.shape, 1)
    out = jnp.zeros_like(qm)
    for h in range(MEM_HEADS):
        msk = (lane >= h * HEAD_DIM) & (lane < (h + 1) * HEAD_DIM)
        qh = jnp.where(msk, qm, 0.0).astype(BF)
        s = lax.dot_general(qh, kk, NT_DIMS, preferred_element_type=F32) * (HEAD_DIM ** -0.5)
        m = jnp.max(s, axis=-1, keepdims=True)
        p = jnp.exp(s - m)
        l = jnp.sum(p, axis=-1, keepdims=True)
        o = jnp.dot(p.astype(BF), vv, preferred_element_type=F32) * (1.0 / l)
        out = jnp.where(msk, o, out)
    return out


def _ffn_kernel(x_ref, gi_ref, go_ref, wg_ref, wu_ref, wd_ref, o_ref):
    x = x_ref[...]
    h = _rms(x, gi_ref[...]).astype(BF)
    g = jnp.dot(h, wg_ref[...], preferred_element_type=F32)
    u = jnp.dot(h, wu_ref[...], preferred_element_type=F32)
    a = (g / (1.0 + jnp.exp(-g)) * u).astype(BF)
    y = jnp.dot(a, wd_ref[...], preferred_element_type=F32)
    o_ref[...] = x + 0.5 * _rms(y, go_ref[...])


def _ffn(x, gi, go, wg, wu, wd):
    row = pl.BlockSpec((TM, D_MODEL), lambda i: (i, 0))
    return pl.pallas_call(
        _ffn_kernel,
        grid=(SEQ // TM,),
        in_specs=[row, _full((1, D_MODEL)), _full((1, D_MODEL)),
                  _full((D_MODEL, D_FF)), _full((D_MODEL, D_FF)), _full((D_FF, D_MODEL))],
        out_specs=row,
        out_shape=jax.ShapeDtypeStruct((SEQ, D_MODEL), F32),
        compiler_params=_cp(1),
        name="ffn",
    )(x, gi, go, wg, wu, wd)


def _memkv_kernel(mem_ref, g_ref, w_ref, k_ref, v_ref):
    mn = _rms(mem_ref[...], g_ref[0]).astype(BF)
    kv = jnp.dot(mn, w_ref[0], preferred_element_type=F32)
    k_ref[0] = kv[:, :MEM_WIDTH].astype(BF)
    v_ref[0] = kv[:, MEM_WIDTH:].astype(BF)


def _memkv(mem, g6, w_kv):
    out = jax.ShapeDtypeStruct((DEPTH, N_MEM, MEM_WIDTH), BF)
    ospec = pl.BlockSpec((1, N_MEM, MEM_WIDTH), lambda i: (i, 0, 0))
    return pl.pallas_call(
        _memkv_kernel,
        grid=(DEPTH,),
        in_specs=[_full((N_MEM, D_MODEL)),
                  pl.BlockSpec((1, 1, D_MODEL), lambda i: (i, 0, 0)),
                  pl.BlockSpec((1, D_MODEL, 2 * MEM_WIDTH), lambda i: (i, 0, 0))],
        out_specs=[ospec, ospec],
        out_shape=[out, out],
        compiler_params=_cp(1),
        name="memkv",
    )(mem, g6, w_kv)


def _out_kernel(o_ref, mo_ref, x_ref, w_ref, g_ref, out_ref, *, wo):
    y = jnp.dot(o_ref[...], w_ref[:wo, :], preferred_element_type=F32)
    y = y + jnp.dot(mo_ref[...], w_ref[wo:, :], preferred_element_type=F32)
    out_ref[...] = x_ref[...] + _rms(y, g_ref[...])


def _out_proj(o, mo, x, w, g):
    wo = o.shape[1]
    row = pl.BlockSpec((TM, D_MODEL), lambda i: (i, 0))
    return pl.pallas_call(
        functools.partial(_out_kernel, wo=wo),
        grid=(SEQ // TM,),
        in_specs=[pl.BlockSpec((TM, wo), lambda i: (i, 0)),
                  pl.BlockSpec((TM, MEM_WIDTH), lambda i: (i, 0)),
                  row, _full(w.shape), _full((1, D_MODEL))],
        out_specs=row,
        out_shape=jax.ShapeDtypeStruct((SEQ, D_MODEL), F32),
        compiler_params=_cp(1),
        name="out_proj",
    )(o, mo, x, w, g)


def _rope(t, cos, sin_signed, lane):
    partner = jnp.where(lane < 80, pltpu.roll(t, LANES - 16, 1), pltpu.roll(t, 16, 1))
    return t * cos + partner * sin_signed


def _mla_in_kernel(x_ref, pos_ref, g_ref, win_ref, qg_ref, kvg_ref, wuq_ref, wuk_ref, wuvt_ref,
                   invf_ref, sgn_ref, mk_ref, mv_ref, q_ref, k_ref, vt_ref, mo_ref):
    h = _rms(x_ref[...], g_ref[...]).astype(BF)
    z = jnp.dot(h, win_ref[...], preferred_element_type=F32)
    o1, o2, o3 = MLA_Q_LORA, MLA_Q_LORA + MLA_KV_LORA, MLA_Q_LORA + MLA_KV_LORA + LANES
    cqn = _rms(z[:, :o1], qg_ref[...]).astype(BF)
    ckvn = _rms(z[:, o1:o2], kvg_ref[...]).astype(BF)
    krg = z[:, o2:o3]
    mo_ref[...] = _mem_attn(z[:, o3:], mk_ref, mv_ref).astype(BF)

    ang = pos_ref[...].astype(F32) * invf_ref[...]
    cos = jnp.cos(ang)
    sin_signed = jnp.sin(ang) * sgn_ref[...]
    lane = lax.broadcasted_iota(jnp.int32, (TM, LANES), 1)

    q = jnp.dot(cqn, wuq_ref[...], preferred_element_type=F32)
    kn = jnp.dot(ckvn, wuk_ref[...], preferred_element_type=F32)
    vt = lax.dot_general(wuvt_ref[...], ckvn, NT_DIMS, preferred_element_type=F32)
    kr = _rope(krg, cos, sin_signed, lane) + (lane == REF_LANE).astype(F32)
    ones_row = (lax.broadcasted_iota(jnp.int32, (LANES, TM), 0) == MLA_V).astype(F32)
    qscale = (MLA_NOPE + MLA_ROPE) ** -0.5 * math.log2(math.e)
    for hd in range(MLA_HEADS):
        sl = slice(hd * LANES, (hd + 1) * LANES)
        q_ref[hd] = (_rope(q[:, sl], cos, sin_signed, lane) * qscale).T.astype(BF)
        k_ref[hd] = (kn[:, sl] + kr).astype(BF)
        vt_ref[hd] = (vt[sl, :] + ones_row).astype(BF)


def _mla_in(x, pos, g, win, qg, kvg, wuq, wuk, wuvt, invf, sgn, mk, mv):
    hm = jax.ShapeDtypeStruct((MLA_HEADS, SEQ, LANES), BF)
    hmt = jax.ShapeDtypeStruct((MLA_HEADS, LANES, SEQ), BF)
    tspec = pl.BlockSpec((MLA_HEADS, LANES, TM), lambda i: (0, 0, i))
    return pl.pallas_call(
        _mla_in_kernel,
        grid=(SEQ // TM,),
        in_specs=[pl.BlockSpec((TM, D_MODEL), lambda i: (i, 0)),
                  pl.BlockSpec((TM, 1), lambda i: (i, 0)),
                  _full((1, D_MODEL)), _full(win.shape), _full(qg.shape), _full(kvg.shape),
                  _full(wuq.shape), _full(wuk.shape), _full(wuvt.shape),
                  _full((1, LANES)), _full((1, LANES)), _full(mk.shape), _full(mv.shape)],
        out_specs=[tspec, pl.BlockSpec((MLA_HEADS, TM, LANES), lambda i: (0, i, 0)), tspec,
                   pl.BlockSpec((TM, MEM_WIDTH), lambda i: (i, 0))],
        out_shape=[hmt, hm, hmt, jax.ShapeDtypeStruct((SEQ, MEM_WIDTH), BF)],
        compiler_params=_cp(1),
        name="mla_in",
    )(x, pos, g, win, qg, kvg, wuq, wuk, wuvt, invf, sgn, mk, mv)


def _flash_kernel(qt_ref, k_ref, vt_ref, o_ref, s00, s01, s10, s11, c00, c01, c10, c11, r0, r1, a0, a1,
                  q0, q1, flag):
    s_sc = ((s00, s01), (s10, s11))
    cm_sc = ((c00, c01), (c10, c11))
    r_sc = (r0, r1)
    acc_sc = (a0, a1)
    qt_sc = (q0, q1)
    n_chunks = SEQ // TK
    pack = 16
    ref_row = lax.broadcasted_iota(jnp.int32, (pack, TQ), 0) == 0

    def scores(c, slot):
        c0 = pl.multiple_of(c * TK, TK)
        worst = None
        for hh in range(2):
            st = jnp.dot(k_ref[hh, pl.ds(c0, TK), :], qt_sc[hh][...], preferred_element_type=F32)
            s_sc[hh][slot][...] = st
            cm = jnp.max(st, axis=0, keepdims=True)
            cm_sc[hh][slot][...] = cm
            top = jnp.max(cm)
            worst = top if worst is None else jnp.maximum(worst, top)
        flag[slot] = (worst > FLASH_REBASE).astype(jnp.int32)

    def rebase(slot, first):
        for hh in range(2):
            r_old = r_sc[hh][...]
            cand = (r_old + cm_sc[hh][slot][...]).astype(BF).astype(F32)
            r_new = cand if first else jnp.maximum(r_old, cand)
            delta = r_new - r_old
            s_sc[hh][slot][...] = s_sc[hh][slot][...] - delta
            if not first:
                acc_sc[hh][...] = acc_sc[hh][...] * jnp.exp2(-delta)
            qt_sc[hh][REF_LANE:REF_LANE + pack, :] = jnp.where(ref_row, -r_new, 0.0).astype(BF)
            r_sc[hh][...] = r_new

    def accumulate(c, slot):
        c0 = pl.multiple_of(c * TK, TK)
        for hh in range(2):
            pt = jnp.exp2(s_sc[hh][slot][...]).astype(BF)
            acc_sc[hh][...] += jnp.dot(vt_ref[hh, :, pl.ds(c0, TK)], pt, preferred_element_type=F32)

    for hh in range(2):
        r_sc[hh][...] = jnp.zeros(r_sc[hh].shape, F32)
        acc_sc[hh][...] = jnp.zeros(acc_sc[hh].shape, F32)
        qt_sc[hh][...] = qt_ref[hh]
    scores(0, 0)
    rebase(0, True)
    flag[0] = 0

    def body(c2, carry):
        c = 2 * c2

        @pl.when(flag[0] != 0)
        def _():
            rebase(0, False)

        scores(c + 1, 1)
        accumulate(c, 0)

        @pl.when(flag[1] != 0)
        def _():
            rebase(1, False)

        scores(jnp.minimum(c + 2, n_chunks - 1), 0)
        accumulate(c + 1, 1)
        return carry

    lax.fori_loop(0, n_chunks // 2, body, 0)
    outs = []
    for hh in range(2):
        acc = acc_sc[hh][...]
        outs.append(acc[:MLA_V] * (1.0 / acc[MLA_V:MLA_V + 1]))
    o_ref[...] = jnp.concatenate(outs, axis=0).T.astype(BF)


def _flash(qt, k, vt):
    return pl.pallas_call(
        _flash_kernel,
        grid=(MLA_HEADS // 2, SEQ // TQ),
        in_specs=[pl.BlockSpec((2, LANES, TQ), lambda p, i: (p, 0, i)),
                  pl.BlockSpec((2, SEQ, LANES), lambda p, i: (p, 0, 0), pipeline_mode=pl.Buffered(1)),
                  pl.BlockSpec((2, LANES, SEQ), lambda p, i: (p, 0, 0), pipeline_mode=pl.Buffered(1))],
        out_specs=pl.BlockSpec((TQ, LANES), lambda p, i: (i, p)),
        out_shape=jax.ShapeDtypeStruct((SEQ, MLA_HEADS * MLA_V), BF),
        scratch_shapes=([pltpu.VMEM((TK, TQ), F32)] * 4 + [pltpu.VMEM((1, TQ), F32)] * 6
                        + [pltpu.VMEM((LANES, TQ), F32)] * 2 + [pltpu.VMEM((LANES, TQ), BF)] * 2
                        + [pltpu.SMEM((2,), jnp.int32)]),
        compiler_params=_cp(2),
        name="mla_flash",
    )(qt, k, vt)


DIL_GW = 3 * DIL_W


def _dil_in_kernel(x_ref, g_ref, w_ref, mk_ref, mv_ref, o0_ref, o1_ref, o2_ref, mo_ref, z1_sc, z2_sc):
    h = _rms(x_ref[...], g_ref[...]).astype(BF)
    outs = (o0_ref, o1_ref, o2_ref)
    stage = (None, z1_sc, z2_sc)
    for gi, (_, d) in enumerate(DIL_GROUPS):
        z = jnp.dot(h, w_ref[:, gi * DIL_GW:(gi + 1) * DIL_GW], preferred_element_type=F32)
        if d == 1:
            outs[gi][0] = z.astype(BF)
        else:
            for c in range(DIL_GW // LANES):
                stage[gi][c] = z[:, c * LANES:(c + 1) * LANES]
            for r in range(d):
                for c in range(DIL_GW // LANES):
                    outs[gi][r, :, c * LANES:(c + 1) * LANES] = (
                        stage[gi][c, pl.ds(r, TM // d, stride=d), :].astype(BF))
    qm = jnp.dot(h, w_ref[:, len(DIL_GROUPS) * DIL_GW:], preferred_element_type=F32)
    mo_ref[...] = _mem_attn(qm, mk_ref, mv_ref).astype(BF)


def _dil_in(x, g, w, mk, mv):
    out_specs, out_shape = [], []
    for _, d in DIL_GROUPS:
        out_specs.append(pl.BlockSpec((d, TM // d, DIL_GW), lambda i: (0, i, 0)))
        out_shape.append(jax.ShapeDtypeStruct((d, SEQ // d, DIL_GW), BF))
    out_specs.append(pl.BlockSpec((TM, MEM_WIDTH), lambda i: (i, 0)))
    out_shape.append(jax.ShapeDtypeStruct((SEQ, MEM_WIDTH), BF))
    return pl.pallas_call(
        _dil_in_kernel,
        grid=(SEQ // TM,),
        in_specs=[pl.BlockSpec((TM, D_MODEL), lambda i: (i, 0)), _full((1, D_MODEL)),
                  _full(w.shape), _full(mk.shape), _full(mv.shape)],
        out_specs=out_specs,
        out_shape=out_shape,
        scratch_shapes=[pltpu.VMEM((DIL_GW // LANES, TM, LANES), F32)] * 2,
        compiler_params=_cp(1),
        name="dil_in",
    )(x, g, w, mk, mv)


def _band_kernel(q_ref, k_ref, v_ref, kp_ref, vp_ref, kn_ref, vn_ref, o_ref, lse_ref, kbuf, vbuf,
                 *, dil, sub_len, slopes):
    j = pl.program_id(1)
    hb = BAND_HALF
    kbuf[0:hb] = kp_ref[...]
    kbuf[hb:hb + BAND_TB] = k_ref[...]
    kbuf[hb + BAND_TB:] = kn_ref[...]
    vbuf[0:hb] = vp_ref[...]
    vbuf[hb:hb + BAND_TB] = v_ref[...]
    vbuf[hb + BAND_TB:] = vn_ref[...]
    win = BAND_SB + 2 * hb
    qi = lax.broadcasted_iota(jnp.int32, (BAND_SB, win), 0)
    kj = lax.broadcasted_iota(jnp.int32, (BAND_SB, win), 1)
    rel = jnp.abs(kj - hb - qi)
    dist = rel.astype(F32) * float(dil)
    lane = lax.broadcasted_iota(jnp.int32, (BAND_SB, LANES), 1)
    for sb in range(BAND_TB // BAND_SB):
        r0 = sb * BAND_SB
        key_idx = j * BAND_TB + (r0 - hb) + kj
        valid = (rel <= hb) & (key_idx >= 0) & (key_idx < sub_len)
        for hp in range(DIL_W // LANES):
            cs = slice(hp * LANES, (hp + 1) * LANES)
            q2 = q_ref[r0:r0 + BAND_SB, cs].astype(F32)
            k2 = kbuf[r0:r0 + win, cs]
            v2 = vbuf[r0:r0 + win, cs]
            o_pair = jnp.zeros((BAND_SB, LANES), F32)
            l_pair = jnp.zeros((BAND_SB, LANES), F32)
            for hh in range(2):
                lm = (lane >= hh * HEAD_DIM) & (lane < (hh + 1) * HEAD_DIM)
                qh = jnp.where(lm, q2, 0.0).astype(BF)
                s = lax.dot_general(qh, k2, NT_DIMS, preferred_element_type=F32) * (HEAD_DIM ** -0.5)
                s = s - slopes[2 * hp + hh] * dist
                s = jnp.where(valid, s, NEG)
                m = jnp.max(s, axis=-1, keepdims=True)
                p = jnp.exp(s - m)
                l = jnp.sum(p, axis=-1, keepdims=True)
                o = jnp.dot(p.astype(BF), v2, preferred_element_type=F32) * (1.0 / l)
                o_pair = jnp.where(lm, o, o_pair)
                l_pair = jnp.where(lm, m + jnp.log(l), l_pair)
            o_ref[r0:r0 + BAND_SB, cs] = o_pair.astype(BF)
            lse_ref[r0:r0 + BAND_SB, cs] = l_pair


def _band_attn(qkv, g_idx, d):
    sub = SEQ // d
    nb = sub // BAND_TB
    hpb = BAND_TB // BAND_HALF
    n_halo = sub // BAND_HALF
    n_heads = len(DIL_GROUPS) * DIL_HEADS
    slopes = tuple(2.0 ** (-ALIBI_MAX * (g_idx * DIL_HEADS + h + 1.0) / n_heads) for h in range(DIL_HEADS))

    def main(col):
        return pl.BlockSpec((None, BAND_TB, DIL_W), lambda r, j: (r, j, col))

    def prev(col):
        return pl.BlockSpec((None, BAND_HALF, DIL_W), lambda r, j: (r, jnp.maximum(j * hpb - 1, 0), col))

    def nxt(col):
        return pl.BlockSpec((None, BAND_HALF, DIL_W),
                            lambda r, j: (r, jnp.minimum((j + 1) * hpb, n_halo - 1), col))

    ospec = pl.BlockSpec((None, BAND_TB, DIL_W), lambda r, j: (r, j, 0))
    return pl.pallas_call(
        functools.partial(_band_kernel, dil=d, sub_len=sub, slopes=slopes),
        grid=(d, nb),
        in_specs=[main(0), main(1), main(2), prev(1), prev(2), nxt(1), nxt(2)],
        out_specs=[ospec, ospec],
        out_shape=[jax.ShapeDtypeStruct((d, sub, DIL_W), BF), jax.ShapeDtypeStruct((d, sub, DIL_W), F32)],
        scratch_shapes=[pltpu.VMEM((BAND_TB + 2 * BAND_HALF, DIL_W), BF),
                        pltpu.VMEM((BAND_TB + 2 * BAND_HALF, DIL_W), BF)],
        compiler_params=_cp(2),
        name=f"band_attn_d{d}",
    )(qkv, qkv, qkv, qkv, qkv, qkv, qkv)


def _dil_out_kernel(o0, o1, o2, l0, l1, l2, mo_ref, x_ref, w_ref, g_ref, out_ref, so1, so2, sl1, sl2):
    nc = DIL_W // LANES

    def token_order(src_ref, stage):
        d = src_ref.shape[0]
        for r in range(d):
            for c in range(nc):
                stage[c, pl.ds(r, TM // d, stride=d), :] = src_ref[r, :, c * LANES:(c + 1) * LANES].astype(F32)
        return jnp.concatenate([stage[c] for c in range(nc)], axis=1)

    a0, a1, a2 = l0[0], token_order(l1, sl1), token_order(l2, sl2)
    mx = jnp.maximum(jnp.maximum(a0, a1), a2)
    e0, e1, e2 = jnp.exp(a0 - mx), jnp.exp(a1 - mx), jnp.exp(a2 - mx)
    inv = 1.0 / (e0 + e1 + e2)
    o = ((e0 * inv) * o0[0].astype(F32) + (e1 * inv) * token_order(o1, so1)
         + (e2 * inv) * token_order(o2, so2))
    y = jnp.dot(o.astype(BF), w_ref[:DIL_W, :], preferred_element_type=F32)
    y = y + jnp.dot(mo_ref[...], w_ref[DIL_W:, :], preferred_element_type=F32)
    out_ref[...] = x_ref[...] + _rms(y, g_ref[...])


def _dil_out(os_, ls_, mo, x, w, g):
    row = pl.BlockSpec((TM, D_MODEL), lambda i: (i, 0))
    grp = [pl.BlockSpec((d, TM // d, DIL_W), lambda i: (0, i, 0)) for _, d in DIL_GROUPS]
    return pl.pallas_call(
        _dil_out_kernel,
        grid=(SEQ // TM,),
        in_specs=grp + grp + [pl.BlockSpec((TM, MEM_WIDTH), lambda i: (i, 0)), row,
                              _full(w.shape), _full((1, D_MODEL))],
        out_specs=row,
        out_shape=jax.ShapeDtypeStruct((SEQ, D_MODEL), F32),
        scratch_shapes=[pltpu.VMEM((DIL_W // LANES, TM, LANES), F32)] * 4,
        compiler_params=_cp(1),
        name="dil_out",
    )(*os_, *ls_, mo, x, w, g)


def _conf_in_kernel(x_ref, g_ref, w_ref, mk_ref, mv_ref, u_ref, mo_ref):
    h = _rms(x_ref[...], g_ref[...]).astype(BF)
    z = jnp.dot(h, w_ref[...], preferred_element_type=F32)
    a, gate = z[:, :CONV_CH], z[:, CONV_CH:2 * CONV_CH]
    u_ref[...] = a / (1.0 + jnp.exp(-gate))
    mo_ref[...] = _mem_attn(z[:, 2 * CONV_CH:], mk_ref, mv_ref).astype(BF)


def _conf_in(x, g, w, mk, mv):
    return pl.pallas_call(
        _conf_in_kernel,
        grid=(SEQ // TM,),
        in_specs=[pl.BlockSpec((TM, D_MODEL), lambda i: (i, 0)), _full((1, D_MODEL)),
                  _full(w.shape), _full(mk.shape), _full(mv.shape)],
        out_specs=[pl.BlockSpec((TM, CONV_CH), lambda i: (i, 0)),
                   pl.BlockSpec((TM, MEM_WIDTH), lambda i: (i, 0))],
        out_shape=[jax.ShapeDtypeStruct((SEQ, CONV_CH), F32), jax.ShapeDtypeStruct((SEQ, MEM_WIDTH), BF)],
        compiler_params=_cp(1),
        name="conf_in",
    )(x, g, w, mk, mv)


def _sc_in_kernel(x_ref, g_ref, w_ref, mk_ref, mv_ref, bg_ref, v_ref, mo_ref):
    h = _rms(x_ref[...], g_ref[...]).astype(BF)
    z = jnp.dot(h, w_ref[...], preferred_element_type=F32)
    bg_ref[...] = z[:, :SC_CH]
    v_ref[...] = z[:, SC_CH:2 * SC_CH] * z[:, 2 * SC_CH:3 * SC_CH]
    mo_ref[...] = _mem_attn(z[:, 3 * SC_CH:], mk_ref, mv_ref).astype(BF)


def _sc_in(x, g, w, mk, mv):
    ch = pl.BlockSpec((TM, SC_CH), lambda i: (i, 0))
    chs = jax.ShapeDtypeStruct((SEQ, SC_CH), F32)
    return pl.pallas_call(
        _sc_in_kernel,
        grid=(SEQ // TM,),
        in_specs=[pl.BlockSpec((TM, D_MODEL), lambda i: (i, 0)), _full((1, D_MODEL)),
                  _full(w.shape), _full(mk.shape), _full(mv.shape)],
        out_specs=[ch, ch, pl.BlockSpec((TM, MEM_WIDTH), lambda i: (i, 0))],
        out_shape=[chs, chs, jax.ShapeDtypeStruct((SEQ, MEM_WIDTH), BF)],
        compiler_params=_cp(1),
        name="sc_in",
    )(x, g, w, mk, mv)


def _fill_window(win, u_ref, up_ref, un_ref, halo):
    i = pl.program_id(0)
    win[0:halo] = jnp.where(i > 0, up_ref[...], 0.0)
    win[halo:halo + TM] = u_ref[...]
    win[halo + TM:] = jnp.where(i < pl.num_programs(0) - 1, un_ref[...], 0.0)


def _dwconv_chunk(win, cw_ref, r0, width, halo):
    base = r0 + halo - width // 2
    acc = cw_ref[0:1, :] * win[base:base + CONV_ROWS, :]
    for k in range(1, width):
        acc = acc + cw_ref[k:k + 1, :] * win[base + k:base + k + CONV_ROWS, :]
    return acc


def _conf_out_kernel(u_ref, up_ref, un_ref, cw_ref, cb_ref, lg_ref, lb_ref, mo_ref, x_ref, w_ref, g_ref,
                     out_ref, win, sh, t_sc, *, halo):
    _fill_window(win, u_ref, up_ref, un_ref, halo)
    first = halo - CONV_WIDTH // 2
    span = sh.shape[1]
    for c in range(TM // CONV_CHUNK):
        r0 = c * CONV_CHUNK
        for b in range(SUBLANES):
            sh[b] = win[r0 + b:r0 + b + span, :]

        def rows(i, carry, r0=r0):
            rr = pl.multiple_of(i * CONV_ROWS, CONV_ROWS)
            t = cb_ref[...]
            for k in range(CONV_WIDTH):
                a, b = divmod(first + k, SUBLANES)
                t = t + cw_ref[k:k + 1, :] * sh[b, pl.ds(rr + SUBLANES * a, CONV_ROWS), :]
            t_sc[pl.ds(r0 + rr, CONV_ROWS), :] = t
            return carry

        lax.fori_loop(0, CONV_CHUNK // CONV_ROWS, rows, 0)
    t = t_sc[...]
    mu = jnp.mean(t, axis=-1, keepdims=True)
    tc = t - mu
    var = jnp.mean(tc * tc, axis=-1, keepdims=True)
    t = tc * lax.rsqrt(var + EPS) * lg_ref[...] + lb_ref[...]
    o = (t / (1.0 + jnp.exp(-t))).astype(BF)
    y = jnp.dot(o, w_ref[:CONV_CH, :], preferred_element_type=F32)
    y = y + jnp.dot(mo_ref[...], w_ref[CONV_CH:, :], preferred_element_type=F32)
    out_ref[...] = x_ref[...] + _rms(y, g_ref[...])


def _sc_out_kernel(v_ref, vp_ref, vn_ref, cw_ref, bg_ref, mo_ref, x_ref, w_ref, g_ref,
                   out_ref, win, o_sc, *, halo):
    _fill_window(win, v_ref, vp_ref, vn_ref, halo)
    for c in range(TM // CONV_ROWS):
        r0 = c * CONV_ROWS
        t = _dwconv_chunk(win, cw_ref, r0, SC_WIDTH, halo)
        o_sc[r0:r0 + CONV_ROWS, :] = (bg_ref[r0:r0 + CONV_ROWS, :] * t).astype(BF)
    y = jnp.dot(o_sc[...], w_ref[:SC_CH, :], preferred_element_type=F32)
    y = y + jnp.dot(mo_ref[...], w_ref[SC_CH:, :], preferred_element_type=F32)
    out_ref[...] = x_ref[...] + _rms(y, g_ref[...])


def _conv_specs(ch, halo):
    hpb = TM // halo
    n_halo = SEQ // halo
    return [pl.BlockSpec((TM, ch), lambda i: (i, 0)),
            pl.BlockSpec((halo, ch), lambda i: (jnp.maximum(i * hpb - 1, 0), 0)),
            pl.BlockSpec((halo, ch), lambda i: (jnp.minimum((i + 1) * hpb, n_halo - 1), 0))]


def _conf_out(u, cw, cb, lg, lb, mo, x, w, g):
    halo = 16
    span_extra = (halo - CONV_WIDTH // 2 + CONV_WIDTH - 1) // SUBLANES * SUBLANES
    row = pl.BlockSpec((TM, D_MODEL), lambda i: (i, 0))
    return pl.pallas_call(
        functools.partial(_conf_out_kernel, halo=halo),
        grid=(SEQ // TM,),
        in_specs=_conv_specs(CONV_CH, halo) + [
            _full(cw.shape), _full(cb.shape), _full(lg.shape), _full(lb.shape),
            pl.BlockSpec((TM, MEM_WIDTH), lambda i: (i, 0)), row, _full(w.shape), _full((1, D_MODEL))],
        out_specs=row,
        out_shape=jax.ShapeDtypeStruct((SEQ, D_MODEL), F32),
        scratch_shapes=[pltpu.VMEM((TM + 2 * halo, CONV_CH), F32),
                        pltpu.VMEM((SUBLANES, CONV_CHUNK + span_extra, CONV_CH), F32),
                        pltpu.VMEM((TM, CONV_CH), F32)],
        compiler_params=_cp(1),
        name="conf_out",
    )(u, u, u, cw, cb, lg, lb, mo, x, w, g)


def _sc_out(v, cw, bg, mo, x, w, g):
    halo = 8
    row = pl.BlockSpec((TM, D_MODEL), lambda i: (i, 0))
    return pl.pallas_call(
        functools.partial(_sc_out_kernel, halo=halo),
        grid=(SEQ // TM,),
        in_specs=_conv_specs(SC_CH, halo) + [
            _full(cw.shape), pl.BlockSpec((TM, SC_CH), lambda i: (i, 0)),
            pl.BlockSpec((TM, MEM_WIDTH), lambda i: (i, 0)), row, _full(w.shape), _full((1, D_MODEL))],
        out_specs=row,
        out_shape=jax.ShapeDtypeStruct((SEQ, D_MODEL), F32),
        scratch_shapes=[pltpu.VMEM((TM + 2 * halo, SC_CH), F32), pltpu.VMEM((TM, SC_CH), BF)],
        compiler_params=_cp(1),
        name="sc_out",
    )(v, v, v, cw, bg, mo, x, w, g)


def _mla_weights(w_in, w_uq, w_ukv):
    o1 = MLA_Q_LORA
    o2 = o1 + MLA_KV_LORA
    o3 = o2 + MLA_ROPE
    zeros = lambda n: jnp.zeros((D_MODEL, n), w_in.dtype)
    win = jnp.concatenate([w_in[:, :o2], zeros(MLA_V), w_in[:, o2:o3], zeros(LANES - MLA_V - MLA_ROPE),
                           w_in[:, o3:]], axis=1).astype(BF)
    qd = MLA_NOPE + MLA_ROPE
    wuq = jnp.pad(w_uq.reshape(MLA_Q_LORA, MLA_HEADS, qd), ((0, 0), (0, 0), (0, LANES - qd)))
    wuq = wuq.reshape(MLA_Q_LORA, MLA_HEADS * LANES).astype(BF)
    wkv = w_ukv.reshape(MLA_KV_LORA, MLA_HEADS, MLA_NOPE + MLA_V)
    pad = ((0, 0), (0, 0), (0, LANES - MLA_NOPE))
    wuk = jnp.pad(wkv[..., :MLA_NOPE], pad).reshape(MLA_KV_LORA, MLA_HEADS * LANES).astype(BF)
    wuv = jnp.pad(wkv[..., MLA_NOPE:], pad).reshape(MLA_KV_LORA, MLA_HEADS * LANES)
    return win, wuq, wuk, wuv.T.astype(BF)


def _rope_tables():
    half = MLA_ROPE // 2
    inv = ROPE_THETA ** (-jnp.arange(half, dtype=F32) / half)
    z = lambda n: jnp.zeros((n,), F32)
    invf = jnp.concatenate([z(MLA_NOPE), inv, inv, z(LANES - MLA_NOPE - MLA_ROPE)])
    sgn = jnp.concatenate([z(MLA_NOPE), -jnp.ones((half,), F32), jnp.ones((half,), F32),
                           z(LANES - MLA_NOPE - MLA_ROPE)])
    return invf[None, :], sgn[None, :]


def kernel(x, mem, positions, norm_g, ffn_w_gate, ffn_w_up, ffn_w_down, mem_w_kv, a_w_in, a_q_norm, a_kv_norm, a_w_uq, a_w_ukv, a_w_out, b_w_in, b_w_out, c_w_in, c_conv_w, c_conv_b, c_ln_g, c_ln_b, c_w_out, d_w_in, d_conv_w, d_w_out):
    assert x.shape == (1, SEQ, D_MODEL) and mem.shape == (1, N_MEM, D_MODEL)
    xs = x[0]
    pos = positions.reshape(SEQ, 1)
    gn = lambda i, j: norm_g[i, j][None, :]
    mem_k, mem_v = _memkv(mem[0], norm_g[:, 6][:, None, :], mem_w_kv.astype(BF))
    invf, sgn = _rope_tables()
    n_mix = 4
    for i in range(DEPTH):
        xs = _ffn(xs, gn(i, 0), gn(i, 1), ffn_w_gate[i, 0].astype(BF), ffn_w_up[i, 0].astype(BF),
                  ffn_w_down[i, 0].astype(BF))
        m, j = i % n_mix, i // n_mix
        mk, mv = mem_k[i], mem_v[i]
        if m == 0:
            win, wuq, wuk, wuvt = _mla_weights(a_w_in[j], a_w_uq[j], a_w_ukv[j])
            q, k, vt, mo = _mla_in(xs, pos, gn(i, 2), win, a_q_norm[j][None, :], a_kv_norm[j][None, :],
                                   wuq, wuk, wuvt, invf, sgn, mk, mv)
            o = _flash(q, k, vt)
            xs = _out_proj(o, mo, xs, a_w_out[j].astype(BF), gn(i, 3))
        elif m == 1:
            *qkvs, mo = _dil_in(xs, gn(i, 2), b_w_in[j].astype(BF), mk, mv)
            os_, ls_ = [], []
            for gi, (_, d) in enumerate(DIL_GROUPS):
                o, lse = _band_attn(qkvs[gi], gi, d)
                os_.append(o)
                ls_.append(lse)
            xs = _dil_out(os_, ls_, mo, xs, b_w_out[j].astype(BF), gn(i, 3))
        elif m == 2:
            u, mo = _conf_in(xs, gn(i, 2), c_w_in[j].astype(BF), mk, mv)
            xs = _conf_out(u, c_conv_w[j], c_conv_b[j][None, :], c_ln_g[j][None, :], c_ln_b[j][None, :],
                           mo, xs, c_w_out[j].astype(BF), gn(i, 3))
        else:
            bg, v, mo = _sc_in(xs, gn(i, 2), d_w_in[j].astype(BF), mk, mv)
            xs = _sc_out(v, d_conv_w[j], bg, mo, xs, d_w_out[j].astype(BF), gn(i, 3))
        xs = _ffn(xs, gn(i, 4), gn(i, 5), ffn_w_gate[i, 1].astype(BF), ffn_w_up[i, 1].astype(BF),
                  ffn_w_down[i, 1].astype(BF))
    return xs[None]
```

```python
import functools
import math

import jax
import jax.numpy as jnp
from jax import lax
from jax.experimental import pallas as pl
from jax.experimental.pallas import tpu as pltpu

F32 = jnp.float32
BF = jnp.bfloat16

D_MODEL = 1024
SEQ = 16384
DEPTH = 4
N_MEM = 256
HEAD_DIM = 64
MEM_HEADS = 4
MEM_WIDTH = MEM_HEADS * HEAD_DIM
MLA_HEADS = 12
MLA_Q_LORA = 384
MLA_KV_LORA = 256
MLA_NOPE = 64
MLA_ROPE = 32
MLA_V = 64
ROPE_THETA = 10000.0
DIL_GROUPS = ((128, 1), (512, 4), (2048, 16))
DIL_HEADS = 8
DIL_W = DIL_HEADS * HEAD_DIM
ALIBI_MAX = 8.0
CONV_CH = 768
CONV_WIDTH = 31
SC_CH = 768
SC_WIDTH = 3
D_FF = 2816
EPS = 1e-6
NEG = -1e30

LANES = 128
SUBLANES = 8
CONV_CHUNK = 128
TM = 512
TQ = 512
TK = 1024
REF_LANE = MLA_NOPE + MLA_ROPE
FLASH_REBASE = 32.0
BAND_TB = 512
BAND_SB = 128
BAND_HALF = 64
CONV_ROWS = 32
VMEM_LIMIT = 56 << 20

NT_DIMS = (((1,), (1,)), ((), ()))


def _cp(n_axes):
    return pltpu.CompilerParams(dimension_semantics=("arbitrary",) * n_axes,
                                vmem_limit_bytes=VMEM_LIMIT)


def _full(shape):
    return pl.BlockSpec(shape, lambda *_: (0,) * len(shape), pipeline_mode=pl.Buffered(1))


def _rms(x, g):
    return x * lax.rsqrt(jnp.mean(x * x, axis=-1, keepdims=True) + EPS) * g


def _mem_attn(qm, mk_ref, mv_ref):
    kk = mk_ref[...]
    vv = mv_ref[...]
    lane = lax.broadcasted_iota(jnp.int32, qm.shape, 1)
    out = jnp.zeros_like(qm)
    for h in range(MEM_HEADS):
        msk = (lane >= h * HEAD_DIM) & (lane < (h + 1) * HEAD_DIM)
        qh = jnp.where(msk, qm, 0.0).astype(BF)
        s = lax.dot_general(qh, kk, NT_DIMS, preferred_element_type=F32) * (HEAD_DIM ** -0.5)
        m = jnp.max(s, axis=-1, keepdims=True)
        p = jnp.exp(s - m)
        l = jnp.sum(p, axis=-1, keepdims=True)
        o = jnp.dot(p.astype(BF), vv, preferred_element_type=F32) * (1.0 / l)
        out = jnp.where(msk, o, out)
    return out


def _ffn_kernel(x_ref, gi_ref, go_ref, wg_ref, wu_ref, wd_ref, o_ref):
    x = x_ref[...]
    h = _rms(x, gi_ref[...]).astype(BF)
    g = jnp.dot(h, wg_ref[...], preferred_element_type=F32)
    u = jnp.dot(h, wu_ref[...], preferred_element_type=F32)
    a = (g / (1.0 + jnp.exp(-g)) * u).astype(BF)
    y = jnp.dot(a, wd_ref[...], preferred_element_type=F32)
    o_ref[...] = x + 0.5 * _rms(y, go_ref[...])


def _ffn(x, gi, go, wg, wu, wd, layer, which):
    row = pl.BlockSpec((TM, D_MODEL), lambda i: (i, 0))

    def pick(rows, cols):
        return pl.BlockSpec((None, None, rows, cols), lambda i: (layer, which, 0, 0),
                            pipeline_mode=pl.Buffered(1))

    return pl.pallas_call(
        _ffn_kernel,
        grid=(SEQ // TM,),
        in_specs=[row, _full((1, D_MODEL)), _full((1, D_MODEL)),
                  pick(D_MODEL, D_FF), pick(D_MODEL, D_FF), pick(D_FF, D_MODEL)],
        out_specs=row,
        out_shape=jax.ShapeDtypeStruct((SEQ, D_MODEL), F32),
        compiler_params=_cp(1),
        name="ffn",
    )(x, gi, go, wg, wu, wd)


def _memkv_kernel(mem_ref, g_ref, w_ref, k_ref, v_ref):
    mn = _rms(mem_ref[...], g_ref[0]).astype(BF)
    kv = jnp.dot(mn, w_ref[0], preferred_element_type=F32)
    k_ref[0] = kv[:, :MEM_WIDTH].astype(BF)
    v_ref[0] = kv[:, MEM_WIDTH:].astype(BF)


def _memkv(mem, g6, w_kv):
    out = jax.ShapeDtypeStruct((DEPTH, N_MEM, MEM_WIDTH), BF)
    ospec = pl.BlockSpec((1, N_MEM, MEM_WIDTH), lambda i: (i, 0, 0))
    return pl.pallas_call(
        _memkv_kernel,
        grid=(DEPTH,),
        in_specs=[_full((N_MEM, D_MODEL)),
                  pl.BlockSpec((1, 1, D_MODEL), lambda i: (i, 0, 0)),
                  pl.BlockSpec((1, D_MODEL, 2 * MEM_WIDTH), lambda i: (i, 0, 0))],
        out_specs=[ospec, ospec],
        out_shape=[out, out],
        compiler_params=_cp(1),
        name="memkv",
    )(mem, g6, w_kv)


def _out_kernel(o_ref, mo_ref, x_ref, w_ref, g_ref, out_ref, *, wo):
    y = jnp.dot(o_ref[...], w_ref[:wo, :], preferred_element_type=F32)
    y = y + jnp.dot(mo_ref[...], w_ref[wo:, :], preferred_element_type=F32)
    out_ref[...] = x_ref[...] + _rms(y, g_ref[...])


def _out_proj(o, mo, x, w, g):
    wo = o.shape[1]
    row = pl.BlockSpec((TM, D_MODEL), lambda i: (i, 0))
    return pl.pallas_call(
        functools.partial(_out_kernel, wo=wo),
        grid=(SEQ // TM,),
        in_specs=[pl.BlockSpec((TM, wo), lambda i: (i, 0)),
                  pl.BlockSpec((TM, MEM_WIDTH), lambda i: (i, 0)),
                  row, _full(w.shape), _full((1, D_MODEL))],
        out_specs=row,
        out_shape=jax.ShapeDtypeStruct((SEQ, D_MODEL), F32),
        compiler_params=_cp(1),
        name="out_proj",
    )(o, mo, x, w, g)


def _rope(t, cos, sin_signed, lane):
    partner = jnp.where(lane < 80, pltpu.roll(t, LANES - 16, 1), pltpu.roll(t, 16, 1))
    return t * cos + partner * sin_signed


def _mla_in_kernel(x_ref, pos_ref, g_ref, win_ref, qg_ref, kvg_ref, wuq_ref, wuk_ref, wuvt_ref,
                   invf_ref, sgn_ref, mk_ref, mv_ref, q_ref, k_ref, vt_ref, mo_ref):
    h = _rms(x_ref[...], g_ref[...]).astype(BF)
    z = jnp.dot(h, win_ref[...], preferred_element_type=F32)
    o1, o2, o3 = MLA_Q_LORA, MLA_Q_LORA + MLA_KV_LORA, MLA_Q_LORA + MLA_KV_LORA + LANES
    cqn = _rms(z[:, :o1], qg_ref[...]).astype(BF)
    ckvn = _rms(z[:, o1:o2], kvg_ref[...]).astype(BF)
    krg = z[:, o2:o3]
    mo_ref[...] = _mem_attn(z[:, o3:], mk_ref, mv_ref).astype(BF)

    ang = pos_ref[...].astype(F32) * invf_ref[...]
    cos = jnp.cos(ang)
    sin_signed = jnp.sin(ang) * sgn_ref[...]
    lane = lax.broadcasted_iota(jnp.int32, (TM, LANES), 1)

    q = jnp.dot(cqn, wuq_ref[...], preferred_element_type=F32)
    kn = jnp.dot(ckvn, wuk_ref[...], preferred_element_type=F32)
    vt = lax.dot_general(wuvt_ref[...], ckvn, NT_DIMS, preferred_element_type=F32)
    kr = _rope(krg, cos, sin_signed, lane) + (lane == REF_LANE).astype(F32)
    ones_row = (lax.broadcasted_iota(jnp.int32, (LANES, TM), 0) == MLA_V).astype(F32)
    qscale = (MLA_NOPE + MLA_ROPE) ** -0.5 * math.log2(math.e)
    for hd in range(MLA_HEADS):
        sl = slice(hd * LANES, (hd + 1) * LANES)
        q_ref[hd] = (_rope(q[:, sl], cos, sin_signed, lane) * qscale).T.astype(BF)
        k_ref[hd] = (kn[:, sl] + kr).astype(BF)
        vt_ref[hd] = (vt[sl, :] + ones_row).astype(BF)


def _mla_in(x, pos, g, win, qg, kvg, wuq, wuk, wuvt, invf, sgn, mk, mv):
    hm = jax.ShapeDtypeStruct((MLA_HEADS, SEQ, LANES), BF)
    hmt = jax.ShapeDtypeStruct((MLA_HEADS, LANES, SEQ), BF)
    tspec = pl.BlockSpec((MLA_HEADS, LANES, TM), lambda i: (0, 0, i))
    return pl.pallas_call(
        _mla_in_kernel,
        grid=(SEQ // TM,),
        in_specs=[pl.BlockSpec((TM, D_MODEL), lambda i: (i, 0)),
                  pl.BlockSpec((TM, 1), lambda i: (i, 0)),
                  _full((1, D_MODEL)), _full(win.shape), _full(qg.shape), _full(kvg.shape),
                  _full(wuq.shape), _full(wuk.shape), _full(wuvt.shape),
                  _full((1, LANES)), _full((1, LANES)), _full(mk.shape), _full(mv.shape)],
        out_specs=[tspec, pl.BlockSpec((MLA_HEADS, TM, LANES), lambda i: (0, i, 0)), tspec,
                   pl.BlockSpec((TM, MEM_WIDTH), lambda i: (i, 0))],
        out_shape=[hmt, hm, hmt, jax.ShapeDtypeStruct((SEQ, MEM_WIDTH), BF)],
        compiler_params=_cp(1),
        name="mla_in",
    )(x, pos, g, win, qg, kvg, wuq, wuk, wuvt, invf, sgn, mk, mv)


def _flash_kernel(qt_ref, k_ref, vt_ref, o_ref, s00, s01, s10, s11, c00, c01, c10, c11, r0, r1, a0, a1,
                  q0, q1, flag):
    s_sc = ((s00, s01), (s10, s11))
    cm_sc = ((c00, c01), (c10, c11))
    r_sc = (r0, r1)
    acc_sc = (a0, a1)
    qt_sc = (q0, q1)
    n_chunks = SEQ // TK
    pack = 16
    ref_row = lax.broadcasted_iota(jnp.int32, (pack, TQ), 0) == 0

    def scores(c, slot):
        c0 = pl.multiple_of(c * TK, TK)
        worst = None
        for hh in range(2):
            st = jnp.dot(k_ref[hh, pl.ds(c0, TK), :], qt_sc[hh][...], preferred_element_type=F32)
            s_sc[hh][slot][...] = st
            cm = jnp.max(st, axis=0, keepdims=True)
            cm_sc[hh][slot][...] = cm
            top = jnp.max(cm)
            worst = top if worst is None else jnp.maximum(worst, top)
        flag[slot] = (worst > FLASH_REBASE).astype(jnp.int32)

    def rebase(slot, first):
        for hh in range(2):
            r_old = r_sc[hh][...]
            cand = (r_old + cm_sc[hh][slot][...]).astype(BF).astype(F32)
            r_new = cand if first else jnp.maximum(r_old, cand)
            delta = r_new - r_old
            s_sc[hh][slot][...] = s_sc[hh][slot][...] - delta
            if not first:
                acc_sc[hh][...] = acc_sc[hh][...] * jnp.exp2(-delta)
            qt_sc[hh][REF_LANE:REF_LANE + pack, :] = jnp.where(ref_row, -r_new, 0.0).astype(BF)
            r_sc[hh][...] = r_new

    def accumulate(c, slot):
        c0 = pl.multiple_of(c * TK, TK)
        for hh in range(2):
            pt = jnp.exp2(s_sc[hh][slot][...]).astype(BF)
            acc_sc[hh][...] += jnp.dot(vt_ref[hh, :, pl.ds(c0, TK)], pt, preferred_element_type=F32)

    for hh in range(2):
        r_sc[hh][...] = jnp.zeros(r_sc[hh].shape, F32)
        acc_sc[hh][...] = jnp.zeros(acc_sc[hh].shape, F32)
        qt_sc[hh][...] = qt_ref[hh]
    scores(0, 0)
    rebase(0, True)
    flag[0] = 0

    def body(c2, carry):
        c = 2 * c2

        @pl.when(flag[0] != 0)
        def _():
            rebase(0, False)

        scores(c + 1, 1)
        accumulate(c, 0)

        @pl.when(flag[1] != 0)
        def _():
            rebase(1, False)

        scores(jnp.minimum(c + 2, n_chunks - 1), 0)
        accumulate(c + 1, 1)
        return carry

    lax.fori_loop(0, n_chunks // 2, body, 0)
    outs = []
    for hh in range(2):
        acc = acc_sc[hh][...]
        outs.append(acc[:MLA_V] * (1.0 / acc[MLA_V:MLA_V + 1]))
    o_ref[...] = jnp.concatenate(outs, axis=0).T.astype(BF)


def _flash(qt, k, vt):
    return pl.pallas_call(
        _flash_kernel,
        grid=(MLA_HEADS // 2, SEQ // TQ),
        in_specs=[pl.BlockSpec((2, LANES, TQ), lambda p, i: (p, 0, i)),
                  pl.BlockSpec((2, SEQ, LANES), lambda p, i: (p, 0, 0), pipeline_mode=pl.Buffered(1)),
                  pl.BlockSpec((2, LANES, SEQ), lambda p, i: (p, 0, 0), pipeline_mode=pl.Buffered(1))],
        out_specs=pl.BlockSpec((TQ, LANES), lambda p, i: (i, p)),
        out_shape=jax.ShapeDtypeStruct((SEQ, MLA_HEADS * MLA_V), BF),
        scratch_shapes=([pltpu.VMEM((TK, TQ), F32)] * 4 + [pltpu.VMEM((1, TQ), F32)] * 6
                        + [pltpu.VMEM((LANES, TQ), F32)] * 2 + [pltpu.VMEM((LANES, TQ), BF)] * 2
                        + [pltpu.SMEM((2,), jnp.int32)]),
        compiler_params=_cp(2),
        name="mla_flash",
    )(qt, k, vt)


DIL_GW = 3 * DIL_W


def _dil_in_kernel(x_ref, g_ref, w_ref, mk_ref, mv_ref, o0_ref, o1_ref, o2_ref, mo_ref, z1_sc, z2_sc):
    h = _rms(x_ref[...], g_ref[...]).astype(BF)
    outs = (o0_ref, o1_ref, o2_ref)
    stage = (None, z1_sc, z2_sc)
    for gi, (_, d) in enumerate(DIL_GROUPS):
        z = jnp.dot(h, w_ref[:, gi * DIL_GW:(gi + 1) * DIL_GW], preferred_element_type=F32)
        if d == 1:
            outs[gi][0] = z.astype(BF)
        else:
            for c in range(DIL_GW // LANES):
                stage[gi][c] = z[:, c * LANES:(c + 1) * LANES]
            for r in range(d):
                for c in range(DIL_GW // LANES):
                    outs[gi][r, :, c * LANES:(c + 1) * LANES] = (
                        stage[gi][c, pl.ds(r, TM // d, stride=d), :].astype(BF))
    qm = jnp.dot(h, w_ref[:, len(DIL_GROUPS) * DIL_GW:], preferred_element_type=F32)
    mo_ref[...] = _mem_attn(qm, mk_ref, mv_ref).astype(BF)


def _dil_in(x, g, w, mk, mv):
    out_specs, out_shape = [], []
    for _, d in DIL_GROUPS:
        out_specs.append(pl.BlockSpec((d, TM // d, DIL_GW), lambda i: (0, i, 0)))
        out_shape.append(jax.ShapeDtypeStruct((d, SEQ // d, DIL_GW), BF))
    out_specs.append(pl.BlockSpec((TM, MEM_WIDTH), lambda i: (i, 0)))
    out_shape.append(jax.ShapeDtypeStruct((SEQ, MEM_WIDTH), BF))
    return pl.pallas_call(
        _dil_in_kernel,
        grid=(SEQ // TM,),
        in_specs=[pl.BlockSpec((TM, D_MODEL), lambda i: (i, 0)), _full((1, D_MODEL)),
                  _full(w.shape), _full(mk.shape), _full(mv.shape)],
        out_specs=out_specs,
        out_shape=out_shape,
        scratch_shapes=[pltpu.VMEM((DIL_GW // LANES, TM, LANES), F32)] * 2,
        compiler_params=_cp(1),
        name="dil_in",
    )(x, g, w, mk, mv)


def _band_kernel(q_ref, k_ref, v_ref, kp_ref, vp_ref, kn_ref, vn_ref, o_ref, lse_ref, kbuf, vbuf,
                 *, dil, sub_len, slopes):
    j = pl.program_id(1)
    hb = BAND_HALF
    kbuf[0:hb] = kp_ref[...]
    kbuf[hb:hb + BAND_TB] = k_ref[...]
    kbuf[hb + BAND_TB:] = kn_ref[...]
    vbuf[0:hb] = vp_ref[...]
    vbuf[hb:hb + BAND_TB] = v_ref[...]
    vbuf[hb + BAND_TB:] = vn_ref[...]
    win = BAND_SB + 2 * hb
    qi = lax.broadcasted_iota(jnp.int32, (BAND_SB, win), 0)
    kj = lax.broadcasted_iota(jnp.int32, (BAND_SB, win), 1)
    rel = jnp.abs(kj - hb - qi)
    dist = rel.astype(F32) * float(dil)
    lane = lax.broadcasted_iota(jnp.int32, (BAND_SB, LANES), 1)
    for sb in range(BAND_TB // BAND_SB):
        r0 = sb * BAND_SB
        key_idx = j * BAND_TB + (r0 - hb) + kj
        valid = (rel <= hb) & (key_idx >= 0) & (key_idx < sub_len)
        for hp in range(DIL_W // LANES):
            cs = slice(hp * LANES, (hp + 1) * LANES)
            q2 = q_ref[r0:r0 + BAND_SB, cs].astype(F32)
            k2 = kbuf[r0:r0 + win, cs]
            v2 = vbuf[r0:r0 + win, cs]
            o_pair = jnp.zeros((BAND_SB, LANES), F32)
            l_pair = jnp.zeros((BAND_SB, LANES), F32)
            for hh in range(2):
                lm = (lane >= hh * HEAD_DIM) & (lane < (hh + 1) * HEAD_DIM)
                qh = jnp.where(lm, q2, 0.0).astype(BF)
                s = lax.dot_general(qh, k2, NT_DIMS, preferred_element_type=F32) * (HEAD_DIM ** -0.5)
                s = s - slopes[2 * hp + hh] * dist
                s = jnp.where(valid, s, NEG)
                m = jnp.max(s, axis=-1, keepdims=True)
                p = jnp.exp(s - m)
                l = jnp.sum(p, axis=-1, keepdims=True)
                o = jnp.dot(p.astype(BF), v2, preferred_element_type=F32) * (1.0 / l)
                o_pair = jnp.where(lm, o, o_pair)
                l_pair = jnp.where(lm, m + jnp.log(l), l_pair)
            o_ref[r0:r0 + BAND_SB, cs] = o_pair.astype(BF)
            lse_ref[r0:r0 + BAND_SB, cs] = l_pair


def _band_attn(qkv, g_idx, d):
    sub = SEQ // d
    nb = sub // BAND_TB
    hpb = BAND_TB // BAND_HALF
    n_halo = sub // BAND_HALF
    n_heads = len(DIL_GROUPS) * DIL_HEADS
    slopes = tuple(2.0 ** (-ALIBI_MAX * (g_idx * DIL_HEADS + h + 1.0) / n_heads) for h in range(DIL_HEADS))

    def main(col):
        return pl.BlockSpec((None, BAND_TB, DIL_W), lambda r, j: (r, j, col))

    def prev(col):
        return pl.BlockSpec((None, BAND_HALF, DIL_W), lambda r, j: (r, jnp.maximum(j * hpb - 1, 0), col))

    def nxt(col):
        return pl.BlockSpec((None, BAND_HALF, DIL_W),
                            lambda r, j: (r, jnp.minimum((j + 1) * hpb, n_halo - 1), col))

    ospec = pl.BlockSpec((None, BAND_TB, DIL_W), lambda r, j: (r, j, 0))
    return pl.pallas_call(
        functools.partial(_band_kernel, dil=d, sub_len=sub, slopes=slopes),
        grid=(d, nb),
        in_specs=[main(0), main(1), main(2), prev(1), prev(2), nxt(1), nxt(2)],
        out_specs=[ospec, ospec],
        out_shape=[jax.ShapeDtypeStruct((d, sub, DIL_W), BF), jax.ShapeDtypeStruct((d, sub, DIL_W), F32)],
        scratch_shapes=[pltpu.VMEM((BAND_TB + 2 * BAND_HALF, DIL_W), BF),
                        pltpu.VMEM((BAND_TB + 2 * BAND_HALF, DIL_W), BF)],
        compiler_params=_cp(2),
        name=f"band_attn_d{d}",
    )(qkv, qkv, qkv, qkv, qkv, qkv, qkv)


def _dil_out_kernel(o0, o1, o2, l0, l1, l2, mo_ref, x_ref, w_ref, g_ref, out_ref, so1, so2, sl1, sl2):
    nc = DIL_W // LANES

    def token_order(src_ref, stage):
        d = src_ref.shape[0]
        for r in range(d):
            for c in range(nc):
                stage[c, pl.ds(r, TM // d, stride=d), :] = src_ref[r, :, c * LANES:(c + 1) * LANES].astype(F32)
        return jnp.concatenate([stage[c] for c in range(nc)], axis=1)

    a0, a1, a2 = l0[0], token_order(l1, sl1), token_order(l2, sl2)
    mx = jnp.maximum(jnp.maximum(a0, a1), a2)
    e0, e1, e2 = jnp.exp(a0 - mx), jnp.exp(a1 - mx), jnp.exp(a2 - mx)
    inv = 1.0 / (e0 + e1 + e2)
    o = ((e0 * inv) * o0[0].astype(F32) + (e1 * inv) * token_order(o1, so1)
         + (e2 * inv) * token_order(o2, so2))
    y = jnp.dot(o.astype(BF), w_ref[:DIL_W, :], preferred_element_type=F32)
    y = y + jnp.dot(mo_ref[...], w_ref[DIL_W:, :], preferred_element_type=F32)
    out_ref[...] = x_ref[...] + _rms(y, g_ref[...])


def _dil_out(os_, ls_, mo, x, w, g):
    row = pl.BlockSpec((TM, D_MODEL), lambda i: (i, 0))
    grp = [pl.BlockSpec((d, TM // d, DIL_W), lambda i: (0, i, 0)) for _, d in DIL_GROUPS]
    return pl.pallas_call(
        _dil_out_kernel,
        grid=(SEQ // TM,),
        in_specs=grp + grp + [pl.BlockSpec((TM, MEM_WIDTH), lambda i: (i, 0)), row,
                              _full(w.shape), _full((1, D_MODEL))],
        out_specs=row,
        out_shape=jax.ShapeDtypeStruct((SEQ, D_MODEL), F32),
        scratch_shapes=[pltpu.VMEM((DIL_W // LANES, TM, LANES), F32)] * 4,
        compiler_params=_cp(1),
        name="dil_out",
    )(*os_, *ls_, mo, x, w, g)


def _conf_in_kernel(x_ref, g_ref, w_ref, mk_ref, mv_ref, u_ref, mo_ref):
    h = _rms(x_ref[...], g_ref[...]).astype(BF)
    z = jnp.dot(h, w_ref[...], preferred_element_type=F32)
    a, gate = z[:, :CONV_CH], z[:, CONV_CH:2 * CONV_CH]
    u_ref[...] = a / (1.0 + jnp.exp(-gate))
    mo_ref[...] = _mem_attn(z[:, 2 * CONV_CH:], mk_ref, mv_ref).astype(BF)


def _conf_in(x, g, w, mk, mv):
    return pl.pallas_call(
        _conf_in_kernel,
        grid=(SEQ // TM,),
        in_specs=[pl.BlockSpec((TM, D_MODEL), lambda i: (i, 0)), _full((1, D_MODEL)),
                  _full(w.shape), _full(mk.shape), _full(mv.shape)],
        out_specs=[pl.BlockSpec((TM, CONV_CH), lambda i: (i, 0)),
                   pl.BlockSpec((TM, MEM_WIDTH), lambda i: (i, 0))],
        out_shape=[jax.ShapeDtypeStruct((SEQ, CONV_CH), F32), jax.ShapeDtypeStruct((SEQ, MEM_WIDTH), BF)],
        compiler_params=_cp(1),
        name="conf_in",
    )(x, g, w, mk, mv)


def _sc_in_kernel(x_ref, g_ref, w_ref, mk_ref, mv_ref, bg_ref, v_ref, mo_ref):
    h = _rms(x_ref[...], g_ref[...]).astype(BF)
    z = jnp.dot(h, w_ref[...], preferred_element_type=F32)
    bg_ref[...] = z[:, :SC_CH]
    v_ref[...] = z[:, SC_CH:2 * SC_CH] * z[:, 2 * SC_CH:3 * SC_CH]
    mo_ref[...] = _mem_attn(z[:, 3 * SC_CH:], mk_ref, mv_ref).astype(BF)


def _sc_in(x, g, w, mk, mv):
    ch = pl.BlockSpec((TM, SC_CH), lambda i: (i, 0))
    chs = jax.ShapeDtypeStruct((SEQ, SC_CH), F32)
    return pl.pallas_call(
        _sc_in_kernel,
        grid=(SEQ // TM,),
        in_specs=[pl.BlockSpec((TM, D_MODEL), lambda i: (i, 0)), _full((1, D_MODEL)),
                  _full(w.shape), _full(mk.shape), _full(mv.shape)],
        out_specs=[ch, ch, pl.BlockSpec((TM, MEM_WIDTH), lambda i: (i, 0))],
        out_shape=[chs, chs, jax.ShapeDtypeStruct((SEQ, MEM_WIDTH), BF)],
        compiler_params=_cp(1),
        name="sc_in",
    )(x, g, w, mk, mv)


def _fill_window(win, u_ref, up_ref, un_ref, halo):
    i = pl.program_id(0)
    win[0:halo] = jnp.where(i > 0, up_ref[...], 0.0)
    win[halo:halo + TM] = u_ref[...]
    win[halo + TM:] = jnp.where(i < pl.num_programs(0) - 1, un_ref[...], 0.0)


def _dwconv_chunk(win, cw_ref, r0, width, halo):
    base = r0 + halo - width // 2
    acc = cw_ref[0:1, :] * win[base:base + CONV_ROWS, :]
    for k in range(1, width):
        acc = acc + cw_ref[k:k + 1, :] * win[base + k:base + k + CONV_ROWS, :]
    return acc


def _conf_out_kernel(u_ref, up_ref, un_ref, cw_ref, cb_ref, lg_ref, lb_ref, mo_ref, x_ref, w_ref, g_ref,
                     out_ref, win, sh, t_sc, *, halo):
    _fill_window(win, u_ref, up_ref, un_ref, halo)
    first = halo - CONV_WIDTH // 2
    span = sh.shape[1]
    for c in range(TM // CONV_CHUNK):
        r0 = c * CONV_CHUNK
        for b in range(SUBLANES):
            sh[b] = win[r0 + b:r0 + b + span, :]

        def rows(i, carry, r0=r0):
            rr = pl.multiple_of(i * CONV_ROWS, CONV_ROWS)
            t = cb_ref[...]
            for k in range(CONV_WIDTH):
                a, b = divmod(first + k, SUBLANES)
                t = t + cw_ref[k:k + 1, :] * sh[b, pl.ds(rr + SUBLANES * a, CONV_ROWS), :]
            t_sc[pl.ds(r0 + rr, CONV_ROWS), :] = t
            return carry

        lax.fori_loop(0, CONV_CHUNK // CONV_ROWS, rows, 0)
    t = t_sc[...]
    mu = jnp.mean(t, axis=-1, keepdims=True)
    tc = t - mu
    var = jnp.mean(tc * tc, axis=-1, keepdims=True)
    t = tc * lax.rsqrt(var + EPS) * lg_ref[...] + lb_ref[...]
    o = (t / (1.0 + jnp.exp(-t))).astype(BF)
    y = jnp.dot(o, w_ref[:CONV_CH, :], preferred_element_type=F32)
    y = y + jnp.dot(mo_ref[...], w_ref[CONV_CH:, :], preferred_element_type=F32)
    out_ref[...] = x_ref[...] + _rms(y, g_ref[...])


def _sc_out_kernel(v_ref, vp_ref, vn_ref, cw_ref, bg_ref, mo_ref, x_ref, w_ref, g_ref,
                   out_ref, win, o_sc, *, halo):
    _fill_window(win, v_ref, vp_ref, vn_ref, halo)
    for c in range(TM // CONV_ROWS):
        r0 = c * CONV_ROWS
        t = _dwconv_chunk(win, cw_ref, r0, SC_WIDTH, halo)
        o_sc[r0:r0 + CONV_ROWS, :] = (bg_ref[r0:r0 + CONV_ROWS, :] * t).astype(BF)
    y = jnp.dot(o_sc[...], w_ref[:SC_CH, :], preferred_element_type=F32)
    y = y + jnp.dot(mo_ref[...], w_ref[SC_CH:, :], preferred_element_type=F32)
    out_ref[...] = x_ref[...] + _rms(y, g_ref[...])


def _conv_specs(ch, halo):
    hpb = TM // halo
    n_halo = SEQ // halo
    return [pl.BlockSpec((TM, ch), lambda i: (i, 0)),
            pl.BlockSpec((halo, ch), lambda i: (jnp.maximum(i * hpb - 1, 0), 0)),
            pl.BlockSpec((halo, ch), lambda i: (jnp.minimum((i + 1) * hpb, n_halo - 1), 0))]


def _conf_out(u, cw, cb, lg, lb, mo, x, w, g):
    halo = 16
    span_extra = (halo - CONV_WIDTH // 2 + CONV_WIDTH - 1) // SUBLANES * SUBLANES
    row = pl.BlockSpec((TM, D_MODEL), lambda i: (i, 0))
    return pl.pallas_call(
        functools.partial(_conf_out_kernel, halo=halo),
        grid=(SEQ // TM,),
        in_specs=_conv_specs(CONV_CH, halo) + [
            _full(cw.shape), _full(cb.shape), _full(lg.shape), _full(lb.shape),
            pl.BlockSpec((TM, MEM_WIDTH), lambda i: (i, 0)), row, _full(w.shape), _full((1, D_MODEL))],
        out_specs=row,
        out_shape=jax.ShapeDtypeStruct((SEQ, D_MODEL), F32),
        scratch_shapes=[pltpu.VMEM((TM + 2 * halo, CONV_CH), F32),
                        pltpu.VMEM((SUBLANES, CONV_CHUNK + span_extra, CONV_CH), F32),
                        pltpu.VMEM((TM, CONV_CH), F32)],
        compiler_params=_cp(1),
        name="conf_out",
    )(u, u, u, cw, cb, lg, lb, mo, x, w, g)


def _sc_out(v, cw, bg, mo, x, w, g):
    halo = 8
    row = pl.BlockSpec((TM, D_MODEL), lambda i: (i, 0))
    return pl.pallas_call(
        functools.partial(_sc_out_kernel, halo=halo),
        grid=(SEQ // TM,),
        in_specs=_conv_specs(SC_CH, halo) + [
            _full(cw.shape), pl.BlockSpec((TM, SC_CH), lambda i: (i, 0)),
            pl.BlockSpec((TM, MEM_WIDTH), lambda i: (i, 0)), row, _full(w.shape), _full((1, D_MODEL))],
        out_specs=row,
        out_shape=jax.ShapeDtypeStruct((SEQ, D_MODEL), F32),
        scratch_shapes=[pltpu.VMEM((TM + 2 * halo, SC_CH), F32), pltpu.VMEM((TM, SC_CH), BF)],
        compiler_params=_cp(1),
        name="sc_out",
    )(v, v, v, cw, bg, mo, x, w, g)


def _mla_weights(w_in, w_uq, w_ukv):
    o1 = MLA_Q_LORA
    o2 = o1 + MLA_KV_LORA
    o3 = o2 + MLA_ROPE
    zeros = lambda n: jnp.zeros((D_MODEL, n), w_in.dtype)
    win = jnp.concatenate([w_in[:, :o2], zeros(MLA_V), w_in[:, o2:o3], zeros(LANES - MLA_V - MLA_ROPE),
                           w_in[:, o3:]], axis=1).astype(BF)
    qd = MLA_NOPE + MLA_ROPE
    wuq = jnp.pad(w_uq.reshape(MLA_Q_LORA, MLA_HEADS, qd), ((0, 0), (0, 0), (0, LANES - qd)))
    wuq = wuq.reshape(MLA_Q_LORA, MLA_HEADS * LANES).astype(BF)
    wkv = w_ukv.reshape(MLA_KV_LORA, MLA_HEADS, MLA_NOPE + MLA_V)
    pad = ((0, 0), (0, 0), (0, LANES - MLA_NOPE))
    wuk = jnp.pad(wkv[..., :MLA_NOPE], pad).reshape(MLA_KV_LORA, MLA_HEADS * LANES).astype(BF)
    wuv = jnp.pad(wkv[..., MLA_NOPE:], pad).reshape(MLA_KV_LORA, MLA_HEADS * LANES)
    return win, wuq, wuk, wuv.T.astype(BF)


def _rope_tables():
    half = MLA_ROPE // 2
    inv = ROPE_THETA ** (-jnp.arange(half, dtype=F32) / half)
    z = lambda n: jnp.zeros((n,), F32)
    invf = jnp.concatenate([z(MLA_NOPE), inv, inv, z(LANES - MLA_NOPE - MLA_ROPE)])
    sgn = jnp.concatenate([z(MLA_NOPE), -jnp.ones((half,), F32), jnp.ones((half,), F32),
                           z(LANES - MLA_NOPE - MLA_ROPE)])
    return invf[None, :], sgn[None, :]


def kernel(x, mem, positions, norm_g, ffn_w_gate, ffn_w_up, ffn_w_down, mem_w_kv, a_w_in, a_q_norm, a_kv_norm, a_w_uq, a_w_ukv, a_w_out, b_w_in, b_w_out, c_w_in, c_conv_w, c_conv_b, c_ln_g, c_ln_b, c_w_out, d_w_in, d_conv_w, d_w_out):
    assert x.shape == (1, SEQ, D_MODEL) and mem.shape == (1, N_MEM, D_MODEL)
    xs = x[0]
    pos = positions.reshape(SEQ, 1)
    gn = lambda i, j: norm_g[i, j][None, :]
    mem_k, mem_v = _memkv(mem[0], norm_g[:, 6][:, None, :], mem_w_kv.astype(BF))
    invf, sgn = _rope_tables()
    wg_all, wu_all, wd_all = ffn_w_gate.astype(BF), ffn_w_up.astype(BF), ffn_w_down.astype(BF)
    n_mix = 4
    for i in range(DEPTH):
        xs = _ffn(xs, gn(i, 0), gn(i, 1), wg_all, wu_all, wd_all, i, 0)
        m, j = i % n_mix, i // n_mix
        mk, mv = mem_k[i], mem_v[i]
        if m == 0:
            win, wuq, wuk, wuvt = _mla_weights(a_w_in[j], a_w_uq[j], a_w_ukv[j])
            q, k, vt, mo = _mla_in(xs, pos, gn(i, 2), win, a_q_norm[j][None, :], a_kv_norm[j][None, :],
                                   wuq, wuk, wuvt, invf, sgn, mk, mv)
            o = _flash(q, k, vt)
            xs = _out_proj(o, mo, xs, a_w_out[j].astype(BF), gn(i, 3))
        elif m == 1:
            *qkvs, mo = _dil_in(xs, gn(i, 2), b_w_in[j].astype(BF), mk, mv)
            os_, ls_ = [], []
            for gi, (_, d) in enumerate(DIL_GROUPS):
                o, lse = _band_attn(qkvs[gi], gi, d)
                os_.append(o)
                ls_.append(lse)
            xs = _dil_out(os_, ls_, mo, xs, b_w_out[j].astype(BF), gn(i, 3))
        elif m == 2:
            u, mo = _conf_in(xs, gn(i, 2), c_w_in[j].astype(BF), mk, mv)
            xs = _conf_out(u, c_conv_w[j], c_conv_b[j][None, :], c_ln_g[j][None, :], c_ln_b[j][None, :],
                           mo, xs, c_w_out[j].astype(BF), gn(i, 3))
        else:
            bg, v, mo = _sc_in(xs, gn(i, 2), d_w_in[j].astype(BF), mk, mv)
            xs = _sc_out(v, d_conv_w[j], bg, mo, xs, d_w_out[j].astype(BF), gn(i, 3))
        xs = _ffn(xs, gn(i, 4), gn(i, 5), wg_all, wu_all, wd_all, i, 1)
    return xs[None]
```
